```python
import numpy as np
import jax
import jax.numpy as jnp
from jax import lax

D_MODEL = 1024
BATCH = 32
SEQ = 2048
DEPTH = 2

HEAD_DIM = 64
RET_WIDTH = D_MODEL // 4
RET_HEADS = RET_WIDTH // HEAD_DIM
RET_CHUNK = 128
CONV_WIDTH = D_MODEL // 4
CONV_K = 3
NSA_WIDTH = D_MODEL // 2
NSA_HEADS = NSA_WIDTH // HEAD_DIM
NSA_KV_GROUPS = 2
NSA_HPG = NSA_HEADS // NSA_KV_GROUPS
NSA_KV_WIDTH = NSA_KV_GROUPS * HEAD_DIM
CMP_LEN = 32
CMP_STRIDE = 16
SLC_LEN = 64
SLC_TOPK = 8
SLC_QBLOCK = 32
WIN_LEN = 256
WIN_QBLOCK = 128
MIX_WIDTH = RET_WIDTH + CONV_WIDTH + NSA_WIDTH
D_FF = 4 * D_MODEL
XPOS_BASE = 10000.0
EPS = 1e-6
NEG_INF = -1e30
FORCE_BONUS = 1e4
IN_WIDTHS = (RET_WIDTH,) * 4 + (CONV_WIDTH,) * 3 + (NSA_WIDTH,) + (NSA_KV_WIDTH,) * 6 + (3 * NSA_HEADS,)
IN_WIDTH = sum(IN_WIDTHS)

kernel_name = 'hybrid_retnet_shortconv_nsa_block'


def rms_norm(x, gain=None):
    xf = x.astype(jnp.float32)
    y = xf * lax.rsqrt(jnp.mean(xf * xf, axis=-1, keepdims=True) + EPS)
    if gain is not None:
        y = y * gain.astype(jnp.float32)
    return y.astype(x.dtype)


def masked_softmax(scores, mask):
    s = jnp.where(mask, scores.astype(jnp.float32), NEG_INF)
    p = jnp.exp(s - jnp.max(s, axis=-1, keepdims=True)) * mask
    return p / jnp.maximum(jnp.sum(p, axis=-1, keepdims=True), 1e-30)


def xpos_rotate(x, cos, sin):
    x1 = x[..., 0::2]
    x2 = x[..., 1::2]
    rot = jnp.stack((-x2, x1), axis=-1).reshape(x.shape)
    return x * cos + rot * sin


def retention(q, k, v, g):
    b, s, _ = q.shape
    h, d, c = RET_HEADS, HEAD_DIM, RET_CHUNK
    n = s // c
    pos = jnp.arange(s, dtype=jnp.float32)
    inv_freq = 1.0 / (XPOS_BASE ** jnp.linspace(0.0, 1.0, d // 2, dtype=jnp.float32))
    ang = jnp.repeat(pos[:, None] * inv_freq[None, :], 2, axis=-1)[:, None, :]
    cos = jnp.cos(ang).astype(q.dtype)
    sin = jnp.sin(ang).astype(q.dtype)
    qh = xpos_rotate(q.reshape(b, s, h, d), cos, sin)
    kh = xpos_rotate(k.reshape(b, s, h, d), cos, sin) * (d ** -0.5)

    def to_chunks(t):
        return t.reshape(b, n, c, h, d).transpose(0, 3, 1, 2, 4)

    qc, kc, vc = to_chunks(qh), to_chunks(kh), to_chunks(v.reshape(b, s, h, d))
    log_gamma = jnp.log(1.0 - 2.0 ** (-5.0 - jnp.arange(h, dtype=jnp.float32)))
    idx = jnp.arange(c, dtype=jnp.float32)
    rel = idx[:, None] - idx[None, :]
    intra_decay = jnp.where(rel >= 0, jnp.exp(log_gamma[:, None, None] * jnp.maximum(rel, 0.0)), 0.0)
    scores = jnp.einsum('bhncd,bhnmd->bhncm', qc, kc) * intra_decay[:, None]
    intra = jnp.einsum('bhncm,bhnmd->bhncd', scores, vc)
    k_decay = jnp.exp(log_gamma[:, None] * (c - 1.0 - idx))[:, None, :, None]
    q_decay = jnp.exp(log_gamma[:, None] * (idx + 1.0))[:, None, :, None]
    chunk_decay = jnp.exp(log_gamma * c)[:, None, None]
    kv = jnp.einsum('bhncd,bhnce->nbhde', kc * k_decay, vc)

    def step(state, kv_n):
        return state * chunk_decay + kv_n, state

    _, prev = lax.scan(step, jnp.zeros(kv.shape[1:], kv.dtype), kv)
    cross = jnp.einsum('bhncd,nbhde->bhnce', qc * q_decay, prev)
    o = (intra + cross).transpose(0, 2, 3, 1, 4).reshape(b, s, h, d)
    o = rms_norm(o).reshape(b, s, h * d)
    return (jax.nn.silu(g) * o).astype(q.dtype)


def short_conv(b_gate, c_gate, h_in, conv_w):
    u = c_gate * h_in
    y = lax.conv_general_dilated(u, conv_w[:, None, :].astype(u.dtype), window_strides=(1,),
                                 padding=[(CONV_K - 1, 0)], dimension_numbers=('NWC', 'WIO', 'NWC'),
                                 feature_group_count=u.shape[-1])
    return b_gate * y


def nsa(q, k_c, v_c, k_s, v_s, k_w, v_w, gate_logits, q_gain, k_gain,
        pe_k, w1_k, w2_k, pe_v, w1_v, w2_v):
    b, s, _ = q.shape
    G, hpg, d = NSA_KV_GROUPS, NSA_HPG, HEAD_DIM
    qh = rms_norm(q.reshape(b, s, G, hpg, d), q_gain) * (d ** -0.5)
    qh = qh.transpose(0, 2, 3, 1, 4)
    t_pos = jnp.arange(s)

    def kv_heads(t):
        return t.reshape(b, s, G, d)

    n_cmp = (s - CMP_LEN) // CMP_STRIDE + 1
    win_idx = np.arange(n_cmp)[:, None] * CMP_STRIDE + np.arange(CMP_LEN)[None, :]

    def compress(t, pe, w1, w2):
        blocks = kv_heads(t)[:, win_idx] + pe[None, None, :, None, :]
        flat = blocks.transpose(0, 1, 3, 2, 4).reshape(b, n_cmp, G, CMP_LEN * d)
        return jax.nn.silu(flat @ w1) @ w2

    k_cmp = rms_norm(compress(k_c, pe_k, w1_k, w2_k), k_gain)
    v_cmp = compress(v_c, pe_v, w1_v, w2_v)
    cmp_end = jnp.arange(n_cmp) * CMP_STRIDE + CMP_LEN - 1
    cmp_mask = cmp_end[None, :] <= t_pos[:, None]
    p_cmp = masked_softmax(jnp.einsum('bghsd,bngd->bghsn', qh, k_cmp), cmp_mask)
    o_cmp = jnp.einsum('bghsn,bngd->bghsd', p_cmp.astype(v_cmp.dtype), v_cmp)

    n_slc = s // SLC_LEN
    n_sel = min(SLC_TOPK, n_slc)
    cs = np.arange(n_cmp)[:, None] * CMP_STRIDE
    ss = np.arange(n_slc)[None, :] * SLC_LEN
    overlap = np.clip(np.minimum(cs + CMP_LEN, ss + SLC_LEN) - np.maximum(cs, ss), 0, None)
    overlap = jnp.asarray(overlap.astype(np.float32) / CMP_LEN)
    imp = jnp.einsum('bghsn,nj->bgsj', p_cmp, overlap)
    t_blk = t_pos // SLC_LEN
    j = jnp.arange(n_slc)
    eligible = j[None, :] <= t_blk[:, None]
    forced = (j[None, :] == 0) | (j[None, :] == t_blk[:, None]) | (j[None, :] == t_blk[:, None] - 1)
    sel_score = jnp.where(eligible, imp + FORCE_BONUS * forced, -1.0)
    _, sel_idx = lax.top_k(sel_score, n_sel)

    ks_blk = rms_norm(kv_heads(k_s), k_gain).transpose(0, 2, 1, 3).reshape(b, G, n_slc, SLC_LEN, d)
    vs_blk = kv_heads(v_s).transpose(0, 2, 1, 3).reshape(b, G, n_slc, SLC_LEN, d)
    gather = jax.vmap(jax.vmap(lambda blocks, ix: blocks[ix]))
    nq = s // SLC_QBLOCK
    n_keys = n_sel * SLC_LEN
    q_sw = qh.reshape(b, G, hpg, nq, SLC_QBLOCK, d).transpose(3, 0, 1, 2, 4, 5)
    idx_sw = sel_idx.reshape(b, G, nq, SLC_QBLOCK, n_sel).transpose(2, 0, 1, 3, 4)

    def sel_block(args):
        q_b, idx_b, blk = args
        t = blk * SLC_QBLOCK + jnp.arange(SLC_QBLOCK)
        k_g = gather(ks_blk, idx_b).reshape(b, G, SLC_QBLOCK, n_keys, d)
        v_g = gather(vs_blk, idx_b).reshape(b, G, SLC_QBLOCK, n_keys, d)
        key_pos = (idx_b[..., None] * SLC_LEN + jnp.arange(SLC_LEN)).reshape(b, G, SLC_QBLOCK, n_keys)
        mask = (key_pos <= t[:, None])[:, :, None]
        p = masked_softmax(jnp.einsum('bghqd,bgqkd->bghqk', q_b, k_g), mask)
        return jnp.einsum('bghqk,bgqkd->bghqd', p.astype(v_g.dtype), v_g)

    o_slc = lax.map(sel_block, (q_sw, idx_sw, jnp.arange(nq)))
    o_slc = o_slc.transpose(1, 2, 3, 0, 4, 5).reshape(b, G, hpg, s, d)

    nb = s // WIN_QBLOCK
    nw = WIN_LEN // WIN_QBLOCK

    def band(t):
        tp = jnp.pad(t.transpose(0, 2, 1, 3), ((0, 0), (0, 0), (WIN_LEN, 0), (0, 0)))
        tp = tp.reshape(b, G, nb + nw, WIN_QBLOCK, d)
        return jnp.concatenate([tp[:, :, i:i + nb] for i in range(nw + 1)], axis=3)

    kw_band = band(rms_norm(kv_heads(k_w), k_gain))
    vw_band = band(kv_heads(v_w))
    q_win = qh.reshape(b, G, hpg, nb, WIN_QBLOCK, d)
    qpos = jnp.arange(nb)[:, None] * WIN_QBLOCK + jnp.arange(WIN_QBLOCK)[None, :]
    kpos = jnp.arange(nb)[:, None] * WIN_QBLOCK - WIN_LEN + jnp.arange((nw + 1) * WIN_QBLOCK)[None, :]
    kp = kpos[:, None, :]
    qp = qpos[:, :, None]
    win_mask = (kp <= qp) & (kp > qp - WIN_LEN) & (kp >= 0)
    p_win = masked_softmax(jnp.einsum('bghiqd,bgikd->bghiqk', q_win, kw_band), win_mask)
    o_win = jnp.einsum('bghiqk,bgikd->bghiqd', p_win.astype(vw_band.dtype), vw_band).reshape(b, G, hpg, s, d)

    gates = jax.nn.sigmoid(gate_logits.astype(jnp.float32)).reshape(b, s, NSA_HEADS, 3).astype(q.dtype)

    def to_bshd(o):
        return o.transpose(0, 3, 1, 2, 4).reshape(b, s, NSA_HEADS, d)

    o = (gates[..., 0:1] * to_bshd(o_cmp) + gates[..., 1:2] * to_bshd(o_slc)
         + gates[..., 2:3] * to_bshd(o_win))
    return o.reshape(b, s, NSA_WIDTH)


def setup_inputs(seed: int = 0) -> dict:
    key = jax.random.key(seed)
    ks = jax.random.split(key, 18)
    f32 = jnp.float32

    def nrm(k, shape, scale):
        return jax.random.normal(k, shape, f32) * scale

    return {
        'x': nrm(ks[0], (BATCH, SEQ, D_MODEL), 1.0),
        'norm_mix': 1.0 + nrm(ks[1], (DEPTH, D_MODEL), 0.05),
        'w_in': nrm(ks[2], (DEPTH, D_MODEL, IN_WIDTH), D_MODEL ** -0.5),
        'conv_w': nrm(ks[3], (DEPTH, CONV_K, CONV_WIDTH), CONV_K ** -0.5),
        'nsa_q_norm': 1.0 + nrm(ks[4], (DEPTH, HEAD_DIM), 0.05),
        'nsa_k_norm': 1.0 + nrm(ks[5], (DEPTH, HEAD_DIM), 0.05),
        'cmp_pe_k': nrm(ks[6], (DEPTH, CMP_LEN, HEAD_DIM), 0.1),
        'cmp_w1_k': nrm(ks[7], (DEPTH, CMP_LEN * HEAD_DIM, HEAD_DIM), (CMP_LEN * HEAD_DIM) ** -0.5),
        'cmp_w2_k': nrm(ks[8], (DEPTH, HEAD_DIM, HEAD_DIM), HEAD_DIM ** -0.5),
        'cmp_pe_v': nrm(ks[9], (DEPTH, CMP_LEN, HEAD_DIM), 0.1),
        'cmp_w1_v': nrm(ks[10], (DEPTH, CMP_LEN * HEAD_DIM, HEAD_DIM), (CMP_LEN * HEAD_DIM) ** -0.5),
        'cmp_w2_v': nrm(ks[11], (DEPTH, HEAD_DIM, HEAD_DIM), HEAD_DIM ** -0.5),
        'w_out': nrm(ks[12], (DEPTH, MIX_WIDTH, D_MODEL), MIX_WIDTH ** -0.5),
        'norm_mlp': 1.0 + nrm(ks[13], (DEPTH, D_MODEL), 0.05),
        'w_up': nrm(ks[14], (DEPTH, D_MODEL, D_FF), D_MODEL ** -0.5),
        'w_down': nrm(ks[15], (DEPTH, D_FF, D_MODEL), D_FF ** -0.5),
    }


def reference(x, norm_mix, w_in, conv_w, nsa_q_norm, nsa_k_norm, cmp_pe_k, cmp_w1_k, cmp_w2_k,
              cmp_pe_v, cmp_w1_v, cmp_w2_v, w_out, norm_mlp, w_up, w_down):
    split_points = [int(v) for v in np.cumsum(IN_WIDTHS)[:-1]]
    for l in range(DEPTH):
        h = rms_norm(x, norm_mix[l])
        z = jnp.einsum('bsd,de->bse', h, w_in[l])
        (r_q, r_k, r_v, r_g, c_b, c_c, c_h, n_q, n_kc, n_vc, n_ks, n_vs, n_kw, n_vw, n_g) = jnp.split(
            z, split_points, axis=-1)
        y_ret = retention(r_q, r_k, r_v, r_g)
        y_conv = short_conv(c_b, c_c, c_h, conv_w[l])
        y_nsa = nsa(n_q, n_kc, n_vc, n_ks, n_vs, n_kw, n_vw, n_g, nsa_q_norm[l], nsa_k_norm[l],
                    cmp_pe_k[l], cmp_w1_k[l], cmp_w2_k[l], cmp_pe_v[l], cmp_w1_v[l], cmp_w2_v[l])
        mixed = jnp.concatenate([y_ret, y_conv, y_nsa], axis=-1)
        x = x + jnp.einsum('bse,ed->bsd', mixed, w_out[l])
        h = rms_norm(x, norm_mlp[l])
        x = x + jnp.einsum('bsf,fd->bsd', jnp.square(jax.nn.relu(jnp.einsum('bsd,df->bsf', h, w_up[l]))), w_down[l])
    return x
```

```python
import functools

import numpy as np
import jax
import jax.numpy as jnp
from jax import lax
from jax.experimental import pallas as pl
from jax.experimental.pallas import tpu as pltpu

F32 = jnp.float32
BF16 = jnp.bfloat16

HEAD_DIM = 64
LANES = 128
RET_CHUNK = 128
RET_HEADS = 4
RET_WIDTH = RET_HEADS * HEAD_DIM
CONV_WIDTH = 256
NSA_HEADS = 8
NSA_GROUPS = 2
NSA_HPG = NSA_HEADS // NSA_GROUPS
NSA_WIDTH = NSA_HEADS * HEAD_DIM
KV_WIDTH = NSA_GROUPS * HEAD_DIM
CMP_LEN = 32
CMP_STRIDE = 16
SLC_LEN = 64
SLC_TOPK = 8
WIN_LEN = 256
Q_TILE = 128
XPOS_BASE = 10000.0
EPS = 1e-6
NEG_INF = -1e30
FORCE_BONUS = 1e4
VMEM_LIMIT = 56 * 1024 * 1024


def _dot(a, b):
    return jnp.dot(a, b, preferred_element_type=F32)


def _dot_nt(a, b):
    return lax.dot_general(a, b, (((1,), (1,)), ((), ())), preferred_element_type=F32)


def _dot_tn(a, b):
    return lax.dot_general(a, b, (((0,), (0,)), ((), ())), preferred_element_type=F32)


def _split_bf16(x):
    hi = x.astype(BF16)
    lo = (x - hi.astype(F32)).astype(BF16)
    return hi, lo


def _dot_hilo(x, w):
    hi, lo = _split_bf16(x)
    return _dot(hi, w) + _dot(lo, w)


def _group_rms(z, bd):
    return z * lax.rsqrt(_dot_hilo(z * z, bd) + EPS)


def _inproj_kernel(x_ref, gain_ref, wrc_ref, wq_ref, wkc_ref, wvc_ref, wkv_ref, wg_ref,
                   qgain_ref, kgain_ref, bd_ref,
                   orc_ref, oq_ref, okc_ref, ovc_ref, okv_ref, og_ref):
    x = x_ref[...]
    h = x * lax.rsqrt(jnp.mean(x * x, axis=-1, keepdims=True) + EPS) * gain_ref[...]
    h = h.astype(BF16)
    orc_ref[...] = _dot(h, wrc_ref[...]).astype(BF16)
    bd = bd_ref[...]
    zq = _dot(h, wq_ref[...])
    qscale = qgain_ref[...] * (HEAD_DIM ** -0.5)
    for c in range(NSA_WIDTH // LANES):
        sl = slice(c * LANES, (c + 1) * LANES)
        oq_ref[:, sl] = (_group_rms(zq[:, sl], bd) * qscale).astype(BF16)
    okc_ref[...] = _dot(h, wkc_ref[...]).astype(BF16)
    ovc_ref[...] = _dot(h, wvc_ref[...]).astype(BF16)
    zkv = _dot(h, wkv_ref[...])
    kgain = kgain_ref[...]
    for c in range(4):
        sl = slice(c * LANES, (c + 1) * LANES)
        if c % 2 == 0:
            okv_ref[:, sl] = (_group_rms(zkv[:, sl], bd) * kgain).astype(BF16)
        else:
            okv_ref[:, sl] = zkv[:, sl].astype(BF16)
    og_ref[...] = _dot(h, wg_ref[...]).astype(BF16)


def _inproj(x2d, gain, wrc, wq, wkc, wvc, wkv, wg, qgain, kgain, bd, tm):
    t, d = x2d.shape
    row = lambda w: pl.BlockSpec((tm, w), lambda i: (i, 0))
    full = lambda a: pl.BlockSpec(a.shape, lambda i: (0,) * a.ndim)
    widths = (wrc.shape[1], wq.shape[1], wkc.shape[1], wvc.shape[1], wkv.shape[1], wg.shape[1])
    return pl.pallas_call(
        _inproj_kernel,
        grid=(t // tm,),
        in_specs=[row(d)] + [full(a) for a in (gain, wrc, wq, wkc, wvc, wkv, wg, qgain, kgain, bd)],
        out_specs=[row(w) for w in widths],
        out_shape=[jax.ShapeDtypeStruct((t, w), BF16) for w in widths],
        compiler_params=pltpu.CompilerParams(dimension_semantics=("arbitrary",), vmem_limit_bytes=VMEM_LIMIT),
        name="inproj",
    )(x2d, gain, wrc, wq, wkc, wvc, wkv, wg, qgain, kgain, bd)


def _retconv_kernel(z_ref, cos_ref, sin_ref, dmask_ref, qdec_ref, kdec_ref, sdec_ref, bdm_ref, bdn_ref, convw_ref,
                    out_ref, state_ref):
    s_len = z_ref.shape[1]
    w = RET_WIDTH
    lane = lax.broadcasted_iota(jnp.int32, (RET_CHUNK, w), 1)
    even = lane % 2 == 0
    head = lane // HEAD_DIM

    def rotate(xb, cos, sin):
        x = xb.astype(F32)
        swapped = jnp.where(even, pltpu.roll(x, w - 1, 1), pltpu.roll(x, 1, 1))
        return x * cos + swapped * sin

    state_ref[...] = jnp.zeros_like(state_ref)

    def chunk(c, carry):
        r0 = pl.multiple_of(c * RET_CHUNK, RET_CHUNK)
        rows = pl.ds(r0, RET_CHUNK)
        cos = cos_ref[rows, :]
        sin = sin_ref[rows, :]
        q = rotate(z_ref[0, rows, 0:w], cos, sin)
        k = rotate(z_ref[0, rows, w:2 * w], cos, sin) * (HEAD_DIM ** -0.5)
        vb = z_ref[0, rows, 2 * w:3 * w]
        gate = z_ref[0, rows, 3 * w:4 * w].astype(F32)
        qb = q.astype(BF16)
        kb = k.astype(BF16)
        zero = jnp.zeros_like(qb)
        o = jnp.zeros((RET_CHUNK, w), F32)
        for h in range(RET_HEADS):
            hm = head == h
            scores = _dot_nt(jnp.where(hm, qb, zero), kb) * dmask_ref[h]
            o = o + _dot(scores.astype(BF16), jnp.where(hm, vb, zero))
        state = state_ref[...]
        o = o + _dot((q * qdec_ref[...]).astype(BF16), state.astype(BF16))
        kv = _dot_tn((k * kdec_ref[...]).astype(BF16), vb)
        state_ref[...] = state * sdec_ref[...] + kv * bdm_ref[...]
        o = _group_rms(o, bdn_ref[...])
        y = gate * jax.nn.sigmoid(gate) * o
        out_ref[0, rows, 0:w] = y.astype(BF16)
        return carry

    lax.fori_loop(0, s_len // RET_CHUNK, chunk, 0)

    c0 = 4 * w
    cw = CONV_WIDTH
    b_gate = z_ref[0, :, c0:c0 + cw].astype(F32)
    u = z_ref[0, :, c0 + cw:c0 + 2 * cw].astype(F32) * z_ref[0, :, c0 + 2 * cw:c0 + 3 * cw].astype(F32)
    t_idx = lax.broadcasted_iota(jnp.int32, (s_len, cw), 0)
    u1 = jnp.where(t_idx >= 1, pltpu.roll(u, 1, 0), 0.0)
    u2 = jnp.where(t_idx >= 2, pltpu.roll(u, 2, 0), 0.0)
    y = b_gate * (convw_ref[0:1, :] * u2 + convw_ref[1:2, :] * u1 + convw_ref[2:3, :] * u)
    out_ref[0, :, w:w + cw] = y.astype(BF16)


def _retconv(z_rc, cos, sin, dmask, qdec, kdec, sdec, bdm, bdn, convw):
    b, s, wz = z_rc.shape
    full = lambda a: pl.BlockSpec(a.shape, lambda i: (0,) * a.ndim)
    return pl.pallas_call(
        _retconv_kernel,
        grid=(b,),
        in_specs=[pl.BlockSpec((1, s, wz), lambda i: (i, 0, 0))]
        + [full(a) for a in (cos, sin, dmask, qdec, kdec, sdec, bdm, bdn, convw)],
        out_specs=pl.BlockSpec((1, s, RET_WIDTH + CONV_WIDTH), lambda i: (i, 0, 0)),
        out_shape=jax.ShapeDtypeStruct((b, s, RET_WIDTH + CONV_WIDTH), BF16),
        scratch_shapes=[pltpu.VMEM((RET_WIDTH, RET_WIDTH), F32)],
        compiler_params=pltpu.CompilerParams(dimension_semantics=("arbitrary",), vmem_limit_bytes=VMEM_LIMIT),
        name="retconv",
    )(z_rc, cos, sin, dmask, qdec, kdec, sdec, bdm, bdn, convw)


def _cmp_kernel(xk_ref, xv_ref, pek_ref, pev_ref, w1k_ref, w1v_ref, w2k_ref, w2v_ref, kgain_ref, bd_ref,
                ok_ref, ov_ref):
    nrow = xk_ref.shape[1]
    row = lax.broadcasted_iota(jnp.int32, (nrow, KV_WIDTH), 0)

    def compress(x_ref, pe_ref, w1_ref, w2_ref):
        x = x_ref[0].astype(F32)
        first = _dot((x + pe_ref[0:1, :]).astype(BF16), w1_ref[0])
        second = _dot((x + pe_ref[1:2, :]).astype(BF16), w1_ref[1])
        pre = first + pltpu.roll(second, nrow - 1, 0)
        return _dot(jax.nn.silu(pre).astype(BF16), w2_ref[...])

    valid = row < nrow - 1
    kc = _group_rms(compress(xk_ref, pek_ref, w1k_ref, w2k_ref), bd_ref[...]) * kgain_ref[...]
    ok_ref[0] = jnp.where(valid, kc, 0.0).astype(BF16)
    vc = compress(xv_ref, pev_ref, w1v_ref, w2v_ref)
    ov_ref[0] = jnp.where(valid, vc, 0.0).astype(BF16)


def _cmp_prep(xk16, xv16, pek, pev, w1k, w1v, w2k, w2v, kgain, bd):
    b, nrow, wide = xk16.shape
    full = lambda a: pl.BlockSpec(a.shape, lambda i: (0,) * a.ndim)
    blk = pl.BlockSpec((1, nrow, wide), lambda i: (i, 0, 0))
    oblk = pl.BlockSpec((1, nrow, KV_WIDTH), lambda i: (i, 0, 0))
    return pl.pallas_call(
        _cmp_kernel,
        grid=(b,),
        in_specs=[blk, blk] + [full(a) for a in (pek, pev, w1k, w1v, w2k, w2v, kgain, bd)],
        out_specs=[oblk, oblk],
        out_shape=[jax.ShapeDtypeStruct((b, nrow, KV_WIDTH), BF16)] * 2,
        compiler_params=pltpu.CompilerParams(dimension_semantics=("arbitrary",), vmem_limit_bytes=VMEM_LIMIT),
        name="cmp_prep",
    )(xk16, xv16, pek, pev, w1k, w1v, w2k, w2v, kgain, bd)


def _nsa_kernel(q_ref, kc_ref, vc_ref, kv_ref, gl_ref, ovt_ref, e3_ref, dupk_ref, dupv_ref, eg_ref,
                out_ref, kdup_ref, vtile_ref, kcdup_ref, vctile_ref, sc_ref, acc_ref, lsum_ref):
    i = pl.program_id(1)
    n_slc = ovt_ref.shape[0]
    tq = Q_TILE
    rows4 = NSA_HPG * tq

    @pl.when(i == 0)
    def _():
        for t in range(2):
            k = kv_ref[0, :, 2 * t * KV_WIDTH:(2 * t + 1) * KV_WIDTH]
            v = kv_ref[0, :, (2 * t + 1) * KV_WIDTH:(2 * t + 2) * KV_WIDTH]
            for g in range(NSA_GROUPS):
                kdup_ref[t, g] = _dot(k, dupk_ref[g]).astype(BF16)
                vtile_ref[t, g] = _dot(v, dupv_ref[g]).astype(BF16)
        for g in range(NSA_GROUPS):
            kcdup_ref[g] = _dot(kc_ref[0], dupk_ref[g]).astype(BF16)
            vctile_ref[g] = _dot(vc_ref[0], dupv_ref[g]).astype(BF16)

    q = q_ref[0]
    lane = lax.broadcasted_iota(jnp.int32, (tq, LANES), 1)
    rowi = lax.broadcasted_iota(jnp.int32, (tq, LANES), 0)
    lane4 = lax.broadcasted_iota(jnp.int32, (rows4, LANES), 1)
    tloc4 = lax.broadcasted_iota(jnp.int32, (rows4, LANES), 0) % tq
    slot = lax.broadcasted_iota(jnp.int32, (tq, NSA_HPG * HEAD_DIM), 1) // HEAD_DIM
    gates = jax.nn.sigmoid(gl_ref[0].astype(F32))
    g_hi, g_lo = _split_bf16(gates)

    def pick(o4):
        y = o4[0:tq]
        for h in range(1, NSA_HPG):
            y = jnp.where(slot == h, o4[h * tq:(h + 1) * tq], y)
        return y

    def rep4(m):
        return jnp.concatenate([m] * NSA_HPG, axis=0)

    for g in range(NSA_GROUPS):
        qs = []
        for h in range(NSA_HPG):
            cb = (NSA_HPG * g + h) // 2
            qp = q[:, cb * LANES:(cb + 1) * LANES]
            keep = (lane >= HEAD_DIM) if h % 2 else (lane < HEAD_DIM)
            qs.append(jnp.where(keep, qp, jnp.zeros_like(qp)))
        q4 = jnp.concatenate(qs, axis=0)

        s = _dot_nt(q4, kcdup_ref[g])
        valid = (lane4 * CMP_STRIDE + (CMP_LEN - 1)) <= (i * tq + tloc4)
        s = jnp.where(valid, s, NEG_INF)
        p = jnp.where(valid, jnp.exp(s - jnp.max(s, axis=-1, keepdims=True)), 0.0)
        p = p * (1.0 / jnp.maximum(jnp.sum(p, axis=-1, keepdims=True), 1e-30))
        o_cmp = pick(_dot(p.astype(BF16), vctile_ref[g]))
        psum = p[0:tq]
        for h in range(1, NSA_HPG):
            psum = psum + p[h * tq:(h + 1) * tq]
        p_hi, p_lo = _split_bf16(psum)
        imp = _dot_nt(ovt_ref[...], p_hi) + _dot_nt(ovt_ref[...], p_lo)

        j_idx = lax.broadcasted_iota(jnp.int32, (n_slc, tq), 0)
        t_blk = (i * tq + lax.broadcasted_iota(jnp.int32, (n_slc, tq), 1)) // SLC_LEN
        forced = (j_idx == 0) | (j_idx == t_blk) | (j_idx == t_blk - 1)
        score = jnp.where(j_idx <= t_blk, imp + jnp.where(forced, FORCE_BONUS, 0.0), -1.0)
        rank = jnp.zeros((n_slc, tq), F32)
        for k in range(n_slc):
            sk = score[k:k + 1, :]
            ahead = (sk > score) | ((sk == score) & (j_idx > k))
            rank = rank + jnp.where(ahead, 1.0, 0.0)
        sel_t = jnp.where(rank < float(SLC_TOPK), 1.0, 0.0).astype(BF16)

        def slc_scores(c, diag):
            keys = pl.ds(pl.multiple_of(c * tq, tq), tq)
            sc = _dot_nt(q4, kdup_ref[0, g, keys, :])
            chosen = _dot_tn(sel_t, e3_ref[c])
            if diag:
                chosen = jnp.where(lane <= rowi, chosen, 0.0)
            return sc + rep4(jnp.where(chosen > 0.5, 0.0, NEG_INF))

        def pass1(c, m):
            sc = slc_scores(c, False)
            sc_ref[c] = sc
            return jnp.maximum(m, sc)

        m = lax.fori_loop(0, i, pass1, jnp.full((rows4, tq), NEG_INF, F32))
        sc = slc_scores(i, True)
        sc_ref[i] = sc
        m_row = jnp.max(jnp.maximum(m, sc), axis=-1, keepdims=True)

        acc_ref[...] = jnp.zeros_like(acc_ref)
        lsum_ref[...] = jnp.zeros_like(lsum_ref)

        def pass2(c, carry):
            keys = pl.ds(pl.multiple_of(c * tq, tq), tq)
            pc = jnp.exp(sc_ref[c] - m_row)
            lsum_ref[...] += pc
            acc_ref[...] += _dot(pc.astype(BF16), vtile_ref[0, g, keys, :])
            return carry

        lax.fori_loop(0, i + 1, pass2, 0)
        l_row = jnp.sum(lsum_ref[...], axis=-1, keepdims=True)
        o_slc = pick(acc_ref[...] * (1.0 / l_row))

        nwk = WIN_LEN + tq
        start = pl.multiple_of(jnp.maximum(i - WIN_LEN // tq, 0) * tq, tq)
        sw = _dot_nt(q4, kdup_ref[1, g, pl.ds(start, nwk), :])
        kp = start + lax.broadcasted_iota(jnp.int32, (rows4, nwk), 1)
        qp = i * tq + lax.broadcasted_iota(jnp.int32, (rows4, nwk), 0) % tq
        okw = (kp <= qp) & (kp > qp - WIN_LEN)
        sw = jnp.where(okw, sw, NEG_INF)
        pw = jnp.exp(sw - jnp.max(sw, axis=-1, keepdims=True))
        lw = jnp.sum(pw, axis=-1, keepdims=True)
        o_win = pick(_dot(pw.astype(BF16), vtile_ref[1, g, pl.ds(start, nwk), :]) * (1.0 / lw))

        y = (_dot(g_hi, eg_ref[0, g]) + _dot(g_lo, eg_ref[0, g])) * o_cmp
        y = y + (_dot(g_hi, eg_ref[1, g]) + _dot(g_lo, eg_ref[1, g])) * o_slc
        y = y + (_dot(g_hi, eg_ref[2, g]) + _dot(g_lo, eg_ref[2, g])) * o_win
        gw = NSA_HPG * HEAD_DIM
        out_ref[0, :, g * gw:(g + 1) * gw] = y.astype(BF16)


def _nsa(qn, kcmp, vcmp, kv4, glog, ovt, e3, dupk, dupv, eg):
    b, s, _ = qn.shape
    nq = s // Q_TILE
    ncmp = kcmp.shape[1]
    gw = NSA_HPG * HEAD_DIM
    full = lambda a: pl.BlockSpec(a.shape, lambda bi, i: (0,) * a.ndim)
    return pl.pallas_call(
        _nsa_kernel,
        grid=(b, nq),
        in_specs=[pl.BlockSpec((1, Q_TILE, NSA_WIDTH), lambda bi, i: (bi, i, 0)),
                  pl.BlockSpec((1, ncmp, KV_WIDTH), lambda bi, i: (bi, 0, 0)),
                  pl.BlockSpec((1, ncmp, KV_WIDTH), lambda bi, i: (bi, 0, 0)),
                  pl.BlockSpec((1, s, 4 * KV_WIDTH), lambda bi, i: (bi, 0, 0)),
                  pl.BlockSpec((1, Q_TILE, LANES), lambda bi, i: (bi, i, 0))]
        + [full(a) for a in (ovt, e3, dupk, dupv, eg)],
        out_specs=pl.BlockSpec((1, Q_TILE, NSA_WIDTH), lambda bi, i: (bi, i, 0)),
        out_shape=jax.ShapeDtypeStruct((b, s, NSA_WIDTH), BF16),
        scratch_shapes=[pltpu.VMEM((2, NSA_GROUPS, s, LANES), BF16),
                        pltpu.VMEM((2, NSA_GROUPS, s, gw), BF16),
                        pltpu.VMEM((NSA_GROUPS, ncmp, LANES), BF16),
                        pltpu.VMEM((NSA_GROUPS, ncmp, gw), BF16),
                        pltpu.VMEM((nq, NSA_HPG * Q_TILE, Q_TILE), F32),
                        pltpu.VMEM((NSA_HPG * Q_TILE, gw), F32),
                        pltpu.VMEM((NSA_HPG * Q_TILE, Q_TILE), F32)],
        compiler_params=pltpu.CompilerParams(dimension_semantics=("arbitrary", "arbitrary"),
                                             vmem_limit_bytes=VMEM_LIMIT),
        name="nsa",
    )(qn, kcmp, vcmp, kv4, glog, ovt, e3, dupk, dupv, eg)


def _outffn_kernel(x_ref, yrc_ref, yn_ref, woa_ref, wob_ref, gain_ref, wup_ref, wdn_ref, out_ref, *, ff_chunk):
    x1 = x_ref[...] + _dot(yrc_ref[...], woa_ref[...]) + _dot(yn_ref[...], wob_ref[...])
    h = x1 * lax.rsqrt(jnp.mean(x1 * x1, axis=-1, keepdims=True) + EPS) * gain_ref[...]
    h = h.astype(BF16)
    out_ref[...] = x1
    for c in range(wup_ref.shape[1] // ff_chunk):
        sl = slice(c * ff_chunk, (c + 1) * ff_chunk)
        u = jnp.maximum(_dot(h, wup_ref[:, sl]), 0.0)
        out_ref[...] += _dot((u * u).astype(BF16), wdn_ref[sl, :])


def _outffn(x2d, yrc, yn, woa, wob, gain, wup, wdn, tm, ff_chunk):
    t, d = x2d.shape
    row = lambda w: pl.BlockSpec((tm, w), lambda i: (i, 0))
    full = lambda a: pl.BlockSpec(a.shape, lambda i: (0,) * a.ndim)
    return pl.pallas_call(
        functools.partial(_outffn_kernel, ff_chunk=ff_chunk),
        grid=(t // tm,),
        in_specs=[row(d), row(yrc.shape[1]), row(yn.shape[1])] + [full(a) for a in (woa, wob, gain, wup, wdn)],
        out_specs=row(d),
        out_shape=jax.ShapeDtypeStruct((t, d), F32),
        compiler_params=pltpu.CompilerParams(dimension_semantics=("arbitrary",), vmem_limit_bytes=VMEM_LIMIT),
        name="outffn",
    )(x2d, yrc, yn, woa, wob, gain, wup, wdn)


def _block_diag_ones(n, block, value):
    idx = np.arange(n) // block
    return jnp.asarray((idx[:, None] == idx[None, :]).astype(np.float32) * value)


def _retention_tables(s):
    h, d, c = RET_HEADS, HEAD_DIM, RET_CHUNK
    pos = jnp.arange(s, dtype=F32)
    inv_freq = 1.0 / (XPOS_BASE ** jnp.linspace(0.0, 1.0, d // 2, dtype=F32))
    ang = jnp.repeat(pos[:, None] * inv_freq[None, :], 2, axis=-1)
    sign = jnp.tile(jnp.asarray([-1.0, 1.0], F32), d // 2)
    cos = jnp.tile(jnp.cos(ang), (1, h))
    sin = jnp.tile(jnp.sin(ang) * sign[None, :], (1, h))
    log_gamma = jnp.log(1.0 - 2.0 ** (-5.0 - jnp.arange(h, dtype=F32)))
    idx = jnp.arange(c, dtype=F32)
    rel = idx[:, None] - idx[None, :]
    dmask = jnp.where(rel >= 0, jnp.exp(log_gamma[:, None, None] * jnp.maximum(rel, 0.0)), 0.0)
    kdec = jnp.repeat(jnp.exp(log_gamma[:, None] * (c - 1.0 - idx)).T, d, axis=1)
    qdec = jnp.repeat(jnp.exp(log_gamma[:, None] * (idx + 1.0)).T, d, axis=1)
    bdm = _block_diag_ones(h * d, d, 1.0)
    sdec = bdm * jnp.repeat(jnp.exp(log_gamma * c), d)[:, None]
    return cos, sin, dmask, qdec, kdec, sdec, bdm


def _nsa_tables(s):
    n_rows = s // CMP_STRIDE
    n_cmp = (s - CMP_LEN) // CMP_STRIDE + 1
    n_slc = s // SLC_LEN
    cs = np.arange(n_cmp)[:, None] * CMP_STRIDE
    ss = np.arange(n_slc)[None, :] * SLC_LEN
    overlap = np.clip(np.minimum(cs + CMP_LEN, ss + SLC_LEN) - np.maximum(cs, ss), 0, None).astype(np.float32) / CMP_LEN
    ovt = np.zeros((n_slc, n_rows), np.float32)
    ovt[:, :n_cmp] = overlap.T
    nq = s // Q_TILE
    key_blk = np.arange(Q_TILE) // SLC_LEN
    e3 = np.zeros((nq, n_slc, Q_TILE), np.float32)
    for c in range(nq):
        e3[c, c * (Q_TILE // SLC_LEN) + key_blk, np.arange(Q_TILE)] = 1.0
    dupk = np.zeros((NSA_GROUPS, KV_WIDTH, LANES), np.float32)
    dupv = np.zeros((NSA_GROUPS, KV_WIDTH, NSA_HPG * HEAD_DIM), np.float32)
    e = np.arange(HEAD_DIM)
    for g in range(NSA_GROUPS):
        for rep in range(LANES // HEAD_DIM):
            dupk[g, g * HEAD_DIM + e, rep * HEAD_DIM + e] = 1.0
        for rep in range(NSA_HPG):
            dupv[g, g * HEAD_DIM + e, rep * HEAD_DIM + e] = 1.0
    eg = np.zeros((3, NSA_GROUPS, LANES, NSA_HPG * HEAD_DIM), np.float32)
    for br in range(3):
        for g in range(NSA_GROUPS):
            for h in range(NSA_HPG):
                eg[br, g, (NSA_HPG * g + h) * 3 + br, h * HEAD_DIM:(h + 1) * HEAD_DIM] = 1.0
    return tuple(jnp.asarray(a, BF16) for a in (ovt, e3, dupk, dupv, eg))


def _cmp_weights(pe, w1, w2):
    half = CMP_LEN // 2
    eye = jnp.eye(NSA_GROUPS, dtype=F32)
    w1r = w1.reshape(2, half, HEAD_DIM, HEAD_DIM)
    w1big = jnp.einsum('arce,gh->argche', w1r, eye).reshape(2, half * KV_WIDTH, KV_WIDTH)
    pe2 = jnp.broadcast_to(pe.reshape(2, half, 1, HEAD_DIM), (2, half, NSA_GROUPS, HEAD_DIM)).reshape(2, half * KV_WIDTH)
    w2big = jnp.kron(eye, w2)
    return pe2, w1big.astype(BF16), w2big.astype(BF16)


def kernel(x, norm_mix, w_in, conv_w, nsa_q_norm, nsa_k_norm, cmp_pe_k, cmp_w1_k, cmp_w2_k,
           cmp_pe_v, cmp_w1_v, cmp_w2_v, w_out, norm_mlp, w_up, w_down):
    b, s, d = x.shape
    t = b * s
    depth = w_in.shape[0]
    tm = 512
    ret_tabs = _retention_tables(s)
    nsa_tabs = _nsa_tables(s)
    bd64 = _block_diag_ones(LANES, HEAD_DIM, 1.0 / HEAD_DIM).astype(BF16)
    bdn = _block_diag_ones(RET_WIDTH, HEAD_DIM, 1.0 / HEAD_DIM).astype(BF16)
    c_rc = 4 * RET_WIDTH + 3 * CONV_WIDTH
    c_q = c_rc + NSA_WIDTH
    c_kc = c_q + KV_WIDTH
    c_vc = c_kc + KV_WIDTH
    c_kv = c_vc + 4 * KV_WIDTH
    n_gate = 3 * NSA_HEADS
    rows16 = s // CMP_STRIDE

    xf = x.reshape(t, d)
    for l in range(depth):
        w = w_in[l].astype(BF16)
        wg = jnp.pad(w[:, c_kv:c_kv + n_gate], ((0, 0), (0, LANES - n_gate)))
        qgain = jnp.tile(nsa_q_norm[l], NSA_HEADS)[None, :LANES]
        kgain = jnp.tile(nsa_k_norm[l], NSA_GROUPS)[None, :]
        z_rc, qn, zkc, zvc, kv4, glog = _inproj(
            xf, norm_mix[l][None, :], w[:, :c_rc], w[:, c_rc:c_q], w[:, c_q:c_kc], w[:, c_kc:c_vc], w[:, c_vc:c_kv], wg,
            qgain, kgain, bd64, tm)
        y_rc = _retconv(z_rc.reshape(b, s, c_rc), *ret_tabs, bdn, conv_w[l])
        pek, w1k, w2k = _cmp_weights(cmp_pe_k[l], cmp_w1_k[l], cmp_w2_k[l])
        pev, w1v, w2v = _cmp_weights(cmp_pe_v[l], cmp_w1_v[l], cmp_w2_v[l])
        kcmp, vcmp = _cmp_prep(zkc.reshape(b, rows16, CMP_STRIDE * KV_WIDTH), zvc.reshape(b, rows16, CMP_STRIDE * KV_WIDTH),
                               pek, pev, w1k, w1v, w2k, w2v, kgain, bd64)
        y_nsa = _nsa(qn.reshape(b, s, NSA_WIDTH), kcmp, vcmp, kv4.reshape(b, s, 4 * KV_WIDTH),
                     glog.reshape(b, s, LANES), *nsa_tabs)
        wo = w_out[l].astype(BF16)
        n_rc = RET_WIDTH + CONV_WIDTH
        xf = _outffn(xf, y_rc.reshape(t, n_rc), y_nsa.reshape(t, NSA_WIDTH), wo[:n_rc], wo[n_rc:],
                     norm_mlp[l][None, :], w_up[l].astype(BF16), w_down[l].astype(BF16), tm, 1024)
    return xf.reshape(b, s, d)
```

```python
import functools

import numpy as np
import jax
import jax.numpy as jnp
from jax import lax
from jax.experimental import pallas as pl
from jax.experimental.pallas import tpu as pltpu

F32 = jnp.float32
BF16 = jnp.bfloat16

HEAD_DIM = 64
LANES = 128
RET_CHUNK = 128
RET_HEADS = 4
RET_WIDTH = RET_HEADS * HEAD_DIM
CONV_WIDTH = 256
NSA_HEADS = 8
NSA_GROUPS = 2
NSA_HPG = NSA_HEADS // NSA_GROUPS
NSA_WIDTH = NSA_HEADS * HEAD_DIM
KV_WIDTH = NSA_GROUPS * HEAD_DIM
CMP_LEN = 32
CMP_STRIDE = 16
SLC_LEN = 64
SLC_TOPK = 8
WIN_LEN = 256
Q_TILE = 128
SLC_CHUNK = 256
XPOS_BASE = 10000.0
EPS = 1e-6
NEG_INF = -1e30
FORCE_BONUS = 1e4
VMEM_LIMIT = 56 * 1024 * 1024


def _dot(a, b):
    return jnp.dot(a, b, preferred_element_type=F32)


def _dot_nt(a, b):
    return lax.dot_general(a, b, (((1,), (1,)), ((), ())), preferred_element_type=F32)


def _dot_tn(a, b):
    return lax.dot_general(a, b, (((0,), (0,)), ((), ())), preferred_element_type=F32)


def _split_bf16(x):
    hi = x.astype(BF16)
    lo = (x - hi.astype(F32)).astype(BF16)
    return hi, lo


def _dot_hilo(x, w):
    hi, lo = _split_bf16(x)
    return _dot(hi, w) + _dot(lo, w)


def _group_rms(z, bd):
    return z * lax.rsqrt(_dot_hilo(z * z, bd) + EPS)


def _inproj_kernel(x_ref, gain_ref, wrc_ref, wq_ref, wkc_ref, wvc_ref, wkv_ref, wg_ref,
                   qgain_ref, kgain_ref, bd_ref,
                   orc_ref, oq_ref, okc_ref, ovc_ref, okv_ref, og_ref):
    x = x_ref[...]
    h = x * lax.rsqrt(jnp.mean(x * x, axis=-1, keepdims=True) + EPS) * gain_ref[...]
    h = h.astype(BF16)
    orc_ref[...] = _dot(h, wrc_ref[...]).astype(BF16)
    bd = bd_ref[...]
    zq = _dot(h, wq_ref[...])
    qscale = qgain_ref[...] * (HEAD_DIM ** -0.5)
    for c in range(NSA_WIDTH // LANES):
        sl = slice(c * LANES, (c + 1) * LANES)
        oq_ref[:, sl] = (_group_rms(zq[:, sl], bd) * qscale).astype(BF16)
    okc_ref[...] = _dot(h, wkc_ref[...]).astype(BF16)
    ovc_ref[...] = _dot(h, wvc_ref[...]).astype(BF16)
    zkv = _dot(h, wkv_ref[...])
    kgain = kgain_ref[...]
    for c in range(4):
        sl = slice(c * LANES, (c + 1) * LANES)
        if c % 2 == 0:
            okv_ref[:, sl] = (_group_rms(zkv[:, sl], bd) * kgain).astype(BF16)
        else:
            okv_ref[:, sl] = zkv[:, sl].astype(BF16)
    og_ref[...] = _dot(h, wg_ref[...]).astype(BF16)


def _inproj(x2d, gain, wrc, wq, wkc, wvc, wkv, wg, qgain, kgain, bd, tm):
    t, d = x2d.shape
    row = lambda w: pl.BlockSpec((tm, w), lambda i: (i, 0))
    full = lambda a: pl.BlockSpec(a.shape, lambda i: (0,) * a.ndim)
    widths = (wrc.shape[1], wq.shape[1], wkc.shape[1], wvc.shape[1], wkv.shape[1], wg.shape[1])
    return pl.pallas_call(
        _inproj_kernel,
        grid=(t // tm,),
        in_specs=[row(d)] + [full(a) for a in (gain, wrc, wq, wkc, wvc, wkv, wg, qgain, kgain, bd)],
        out_specs=[row(w) for w in widths],
        out_shape=[jax.ShapeDtypeStruct((t, w), BF16) for w in widths],
        compiler_params=pltpu.CompilerParams(dimension_semantics=("arbitrary",), vmem_limit_bytes=VMEM_LIMIT),
        name="inproj",
    )(x2d, gain, wrc, wq, wkc, wvc, wkv, wg, qgain, kgain, bd)


def _retconv_kernel(z_ref, cos_ref, sin_ref, dmask_ref, qdec_ref, kdec_ref, sdec_ref, bdm_ref, bdn_ref, convw_ref,
                    out_ref, state_ref):
    s_len = z_ref.shape[1]
    w = RET_WIDTH
    lane = lax.broadcasted_iota(jnp.int32, (RET_CHUNK, w), 1)
    even = lane % 2 == 0
    head = lane // HEAD_DIM

    def rotate(xb, cos, sin):
        x = xb.astype(F32)
        swapped = jnp.where(even, pltpu.roll(x, w - 1, 1), pltpu.roll(x, 1, 1))
        return x * cos + swapped * sin

    state_ref[...] = jnp.zeros_like(state_ref)

    def chunk(c, carry):
        r0 = pl.multiple_of(c * RET_CHUNK, RET_CHUNK)
        rows = pl.ds(r0, RET_CHUNK)
        cos = cos_ref[rows, :]
        sin = sin_ref[rows, :]
        q = rotate(z_ref[0, rows, 0:w], cos, sin)
        k = rotate(z_ref[0, rows, w:2 * w], cos, sin) * (HEAD_DIM ** -0.5)
        vb = z_ref[0, rows, 2 * w:3 * w]
        gate = z_ref[0, rows, 3 * w:4 * w].astype(F32)
        qb = q.astype(BF16)
        kb = k.astype(BF16)
        zero = jnp.zeros_like(qb)
        o = jnp.zeros((RET_CHUNK, w), F32)
        for h in range(RET_HEADS):
            hm = head == h
            scores = _dot_nt(jnp.where(hm, qb, zero), kb) * dmask_ref[h]
            o = o + _dot(scores.astype(BF16), jnp.where(hm, vb, zero))
        state = state_ref[...]
        o = o + _dot((q * qdec_ref[...]).astype(BF16), state.astype(BF16))
        kv = _dot_tn((k * kdec_ref[...]).astype(BF16), vb)
        state_ref[...] = state * sdec_ref[...] + kv * bdm_ref[...]
        o = _group_rms(o, bdn_ref[...])
        y = gate * jax.nn.sigmoid(gate) * o
        out_ref[0, rows, 0:w] = y.astype(BF16)
        return carry

    lax.fori_loop(0, s_len // RET_CHUNK, chunk, 0)

    c0 = 4 * w
    cw = CONV_WIDTH
    b_gate = z_ref[0, :, c0:c0 + cw].astype(F32)
    u = z_ref[0, :, c0 + cw:c0 + 2 * cw].astype(F32) * z_ref[0, :, c0 + 2 * cw:c0 + 3 * cw].astype(F32)
    t_idx = lax.broadcasted_iota(jnp.int32, (s_len, cw), 0)
    u1 = jnp.where(t_idx >= 1, pltpu.roll(u, 1, 0), 0.0)
    u2 = jnp.where(t_idx >= 2, pltpu.roll(u, 2, 0), 0.0)
    y = b_gate * (convw_ref[0:1, :] * u2 + convw_ref[1:2, :] * u1 + convw_ref[2:3, :] * u)
    out_ref[0, :, w:w + cw] = y.astype(BF16)


def _retconv(z_rc, cos, sin, dmask, qdec, kdec, sdec, bdm, bdn, convw):
    b, s, wz = z_rc.shape
    full = lambda a: pl.BlockSpec(a.shape, lambda i: (0,) * a.ndim)
    return pl.pallas_call(
        _retconv_kernel,
        grid=(b,),
        in_specs=[pl.BlockSpec((1, s, wz), lambda i: (i, 0, 0))]
        + [full(a) for a in (cos, sin, dmask, qdec, kdec, sdec, bdm, bdn, convw)],
        out_specs=pl.BlockSpec((1, s, RET_WIDTH + CONV_WIDTH), lambda i: (i, 0, 0)),
        out_shape=jax.ShapeDtypeStruct((b, s, RET_WIDTH + CONV_WIDTH), BF16),
        scratch_shapes=[pltpu.VMEM((RET_WIDTH, RET_WIDTH), F32)],
        compiler_params=pltpu.CompilerParams(dimension_semantics=("arbitrary",), vmem_limit_bytes=VMEM_LIMIT),
        name="retconv",
    )(z_rc, cos, sin, dmask, qdec, kdec, sdec, bdm, bdn, convw)


def _cmp_kernel(xk_ref, xv_ref, pek_ref, pev_ref, w1k_ref, w1v_ref, w2k_ref, w2v_ref, kgain_ref, bd_ref,
                ok_ref, ov_ref):
    nrow = xk_ref.shape[1]
    row = lax.broadcasted_iota(jnp.int32, (nrow, KV_WIDTH), 0)

    def compress(x_ref, pe_ref, w1_ref, w2_ref):
        x = x_ref[0].astype(F32)
        first = _dot((x + pe_ref[0:1, :]).astype(BF16), w1_ref[0])
        second = _dot((x + pe_ref[1:2, :]).astype(BF16), w1_ref[1])
        pre = first + pltpu.roll(second, nrow - 1, 0)
        return _dot(jax.nn.silu(pre).astype(BF16), w2_ref[...])

    valid = row < nrow - 1
    kc = _group_rms(compress(xk_ref, pek_ref, w1k_ref, w2k_ref), bd_ref[...]) * kgain_ref[...]
    ok_ref[0] = jnp.where(valid, kc, 0.0).astype(BF16)
    vc = compress(xv_ref, pev_ref, w1v_ref, w2v_ref)
    ov_ref[0] = jnp.where(valid, vc, 0.0).astype(BF16)


def _cmp_prep(xk16, xv16, pek, pev, w1k, w1v, w2k, w2v, kgain, bd):
    b, nrow, wide = xk16.shape
    full = lambda a: pl.BlockSpec(a.shape, lambda i: (0,) * a.ndim)
    blk = pl.BlockSpec((1, nrow, wide), lambda i: (i, 0, 0))
    oblk = pl.BlockSpec((1, nrow, KV_WIDTH), lambda i: (i, 0, 0))
    return pl.pallas_call(
        _cmp_kernel,
        grid=(b,),
        in_specs=[blk, blk] + [full(a) for a in (pek, pev, w1k, w1v, w2k, w2v, kgain, bd)],
        out_specs=[oblk, oblk],
        out_shape=[jax.ShapeDtypeStruct((b, nrow, KV_WIDTH), BF16)] * 2,
        compiler_params=pltpu.CompilerParams(dimension_semantics=("arbitrary",), vmem_limit_bytes=VMEM_LIMIT),
        name="cmp_prep",
    )(xk16, xv16, pek, pev, w1k, w1v, w2k, w2v, kgain, bd)


def _nsa_kernel(q_ref, kc_ref, vc_ref, kv_ref, gl_ref, ovt_ref, eye_ref, e3_ref, dupk_ref, dupv_ref, eg_ref,
                out_ref, kdup_ref, vtile_ref, kcdup_ref, vctile_ref, sc_ref, m_ref, acc_ref, lsum_ref):
    i = pl.program_id(1)
    n_slc = ovt_ref.shape[0]
    tq = Q_TILE
    kc = SLC_CHUNK
    rows4 = NSA_HPG * tq
    gw = NSA_HPG * HEAD_DIM
    groups = range(NSA_GROUPS)

    @pl.when(i == 0)
    def _():
        for t in range(2):
            k = kv_ref[0, :, 2 * t * KV_WIDTH:(2 * t + 1) * KV_WIDTH]
            v = kv_ref[0, :, (2 * t + 1) * KV_WIDTH:(2 * t + 2) * KV_WIDTH]
            for g in groups:
                kdup_ref[t, g] = _dot(k, dupk_ref[g]).astype(BF16)
                vtile_ref[t, g] = _dot(v, dupv_ref[g]).astype(BF16)
        for g in groups:
            kcdup_ref[g] = _dot(kc_ref[0], dupk_ref[g]).astype(BF16)
            vctile_ref[g] = _dot(vc_ref[0], dupv_ref[g]).astype(BF16)

    q = q_ref[0]
    lane = lax.broadcasted_iota(jnp.int32, (tq, LANES), 1)
    lane4 = lax.broadcasted_iota(jnp.int32, (rows4, LANES), 1)
    tloc4 = lax.broadcasted_iota(jnp.int32, (rows4, LANES), 0) % tq
    slot = lax.broadcasted_iota(jnp.int32, (tq, gw), 1) // HEAD_DIM

    def pick(o4):
        y = o4[0:tq]
        for h in range(1, NSA_HPG):
            y = jnp.where(slot == h, o4[h * tq:(h + 1) * tq], y)
        return y

    def rep4(m):
        return jnp.concatenate([m] * NSA_HPG, axis=0)

    gates = jax.nn.sigmoid(gl_ref[0].astype(F32))
    g_hi, g_lo = _split_bf16(gates)
    gexp2 = _dot(jnp.concatenate([g_hi, g_lo], axis=0), eg_ref[...])
    gexp = gexp2[0:tq] + gexp2[tq:2 * tq]

    q4s, o_cmps, selrows = [], [], []
    for g in groups:
        qs = []
        for h in range(NSA_HPG):
            cb = (NSA_HPG * g + h) // 2
            qp = q[:, cb * LANES:(cb + 1) * LANES]
            keep = (lane >= HEAD_DIM) if h % 2 else (lane < HEAD_DIM)
            qs.append(jnp.where(keep, qp, jnp.zeros_like(qp)))
        q4 = jnp.concatenate(qs, axis=0)
        q4s.append(q4)

        s = _dot_nt(q4, kcdup_ref[g])
        valid = (lane4 * CMP_STRIDE + (CMP_LEN - 1)) <= (i * tq + tloc4)
        s = jnp.where(valid, s, NEG_INF)
        p = jnp.where(valid, jnp.exp(s - jnp.max(s, axis=-1, keepdims=True)), 0.0)
        p = p * (1.0 / jnp.maximum(jnp.sum(p, axis=-1, keepdims=True), 1e-30))
        o_cmps.append(pick(_dot(p.astype(BF16), vctile_ref[g])))
        psum = p[0:tq]
        for h in range(1, NSA_HPG):
            psum = psum + p[h * tq:(h + 1) * tq]
        p_hi, p_lo = _split_bf16(psum)
        imp = _dot_nt(ovt_ref[...], p_hi) + _dot_nt(ovt_ref[...], p_lo)

        j_idx = lax.broadcasted_iota(jnp.int32, (n_slc, tq), 0)
        t_blk = (i * tq + lax.broadcasted_iota(jnp.int32, (n_slc, tq), 1)) // SLC_LEN
        forced = (j_idx == 0) | (j_idx == t_blk) | (j_idx == t_blk - 1)
        score = jnp.where(j_idx <= t_blk, imp + jnp.where(forced, FORCE_BONUS, 0.0), -1.0)
        rank = jnp.zeros((n_slc, tq), F32)
        for k in range(n_slc):
            sk = score[k:k + 1, :]
            ahead = (sk > score) | ((sk == score) & (j_idx > k))
            rank = rank + jnp.where(ahead, 1.0, 0.0)
        sel_t = jnp.where(rank < float(SLC_TOPK), 1.0, 0.0).astype(BF16)
        selrows.append(_dot_tn(sel_t, eye_ref[...]).astype(BF16))

    n_chunks = (i * tq) // kc + 1
    m_ref[...] = jnp.full(m_ref.shape, NEG_INF, F32)
    acc_ref[...] = jnp.zeros_like(acc_ref)
    lsum_ref[...] = jnp.zeros_like(lsum_ref)
    tpos = i * tq + lax.broadcasted_iota(jnp.int32, (tq, kc), 0)
    klane = lax.broadcasted_iota(jnp.int32, (tq, kc), 1)

    def pass1(c, carry):
        keys = pl.ds(pl.multiple_of(c * kc, kc), kc)
        causal = (c * kc + klane) <= tpos
        for g in groups:
            chosen = jnp.where(causal, _dot(selrows[g], e3_ref[c]), 0.0)
            sc = _dot_nt(q4s[g], kdup_ref[0, g, keys, :]) + rep4(jnp.where(chosen > 0.5, 0.0, NEG_INF))
            sc_ref[g, c] = sc
            cmax = sc[:, 0:LANES]
            for j in range(1, kc // LANES):
                cmax = jnp.maximum(cmax, sc[:, j * LANES:(j + 1) * LANES])
            m_ref[g] = jnp.maximum(m_ref[g], cmax)
        return carry

    lax.fori_loop(0, n_chunks, pass1, 0)
    m_rows = [jnp.max(m_ref[g], axis=-1, keepdims=True) for g in groups]

    def pass2(c, carry):
        keys = pl.ds(pl.multiple_of(c * kc, kc), kc)
        for g in groups:
            pc = jnp.exp(sc_ref[g, c] - m_rows[g])
            psum = pc[:, 0:LANES]
            for j in range(1, kc // LANES):
                psum = psum + pc[:, j * LANES:(j + 1) * LANES]
            lsum_ref[g] += psum
            acc_ref[g] += _dot(pc.astype(BF16), vtile_ref[0, g, keys, :])
        return carry

    lax.fori_loop(0, n_chunks, pass2, 0)

    nwk = WIN_LEN + tq
    start = pl.multiple_of(jnp.maximum(i - WIN_LEN // tq, 0) * tq, tq)
    kp = start + lax.broadcasted_iota(jnp.int32, (tq, nwk), 1)
    qp = i * tq + lax.broadcasted_iota(jnp.int32, (tq, nwk), 0)
    wbias = rep4(jnp.where(kp <= qp, jnp.where(kp > qp - WIN_LEN, 0.0, NEG_INF), NEG_INF))
    for g in groups:
        l_row = jnp.sum(lsum_ref[g], axis=-1, keepdims=True)
        o_slc = pick(acc_ref[g] * (1.0 / l_row))

        sw = _dot_nt(q4s[g], kdup_ref[1, g, pl.ds(start, nwk), :]) + wbias
        pw = jnp.exp(sw - jnp.max(sw, axis=-1, keepdims=True))
        lw = jnp.sum(pw, axis=-1, keepdims=True)
        o_win = pick(_dot(pw.astype(BF16), vtile_ref[1, g, pl.ds(start, nwk), :]) * (1.0 / lw))

        y = gexp[:, (0 * NSA_GROUPS + g) * gw:(0 * NSA_GROUPS + g + 1) * gw] * o_cmps[g]
        y = y + gexp[:, (1 * NSA_GROUPS + g) * gw:(1 * NSA_GROUPS + g + 1) * gw] * o_slc
        y = y + gexp[:, (2 * NSA_GROUPS + g) * gw:(2 * NSA_GROUPS + g + 1) * gw] * o_win
        out_ref[0, :, g * gw:(g + 1) * gw] = y.astype(BF16)


def _nsa(qn, kcmp, vcmp, kv4, glog, ovt, eye, e3, dupk, dupv, eg):
    b, s, _ = qn.shape
    nq = s // Q_TILE
    ncmp = kcmp.shape[1]
    gw = NSA_HPG * HEAD_DIM
    rows4 = NSA_HPG * Q_TILE
    full = lambda a: pl.BlockSpec(a.shape, lambda bi, i: (0,) * a.ndim)
    return pl.pallas_call(
        _nsa_kernel,
        grid=(b, nq),
        in_specs=[pl.BlockSpec((1, Q_TILE, NSA_WIDTH), lambda bi, i: (bi, i, 0)),
                  pl.BlockSpec((1, ncmp, KV_WIDTH), lambda bi, i: (bi, 0, 0)),
                  pl.BlockSpec((1, ncmp, KV_WIDTH), lambda bi, i: (bi, 0, 0)),
                  pl.BlockSpec((1, s, 4 * KV_WIDTH), lambda bi, i: (bi, 0, 0)),
                  pl.BlockSpec((1, Q_TILE, LANES), lambda bi, i: (bi, i, 0))]
        + [full(a) for a in (ovt, eye, e3, dupk, dupv, eg)],
        out_specs=pl.BlockSpec((1, Q_TILE, NSA_WIDTH), lambda bi, i: (bi, i, 0)),
        out_shape=jax.ShapeDtypeStruct((b, s, NSA_WIDTH), BF16),
        scratch_shapes=[pltpu.VMEM((2, NSA_GROUPS, s, LANES), BF16),
                        pltpu.VMEM((2, NSA_GROUPS, s, gw), BF16),
                        pltpu.VMEM((NSA_GROUPS, ncmp, LANES), BF16),
                        pltpu.VMEM((NSA_GROUPS, ncmp, gw), BF16),
                        pltpu.VMEM((NSA_GROUPS, s // SLC_CHUNK, rows4, SLC_CHUNK), F32),
                        pltpu.VMEM((NSA_GROUPS, rows4, LANES), F32),
                        pltpu.VMEM((NSA_GROUPS, rows4, gw), F32),
                        pltpu.VMEM((NSA_GROUPS, rows4, LANES), F32)],
        compiler_params=pltpu.CompilerParams(dimension_semantics=("arbitrary", "arbitrary"),
                                             vmem_limit_bytes=VMEM_LIMIT),
        name="nsa",
    )(qn, kcmp, vcmp, kv4, glog, ovt, eye, e3, dupk, dupv, eg)


def _outffn_kernel(x_ref, yrc_ref, yn_ref, woa_ref, wob_ref, gain_ref, wup_ref, wdn_ref, out_ref, *, ff_chunk):
    x1 = x_ref[...] + _dot(yrc_ref[...], woa_ref[...]) + _dot(yn_ref[...], wob_ref[...])
    h = x1 * lax.rsqrt(jnp.mean(x1 * x1, axis=-1, keepdims=True) + EPS) * gain_ref[...]
    h = h.astype(BF16)
    out_ref[...] = x1
    for c in range(wup_ref.shape[1] // ff_chunk):
        sl = slice(c * ff_chunk, (c + 1) * ff_chunk)
        u = jnp.maximum(_dot(h, wup_ref[:, sl]), 0.0)
        out_ref[...] += _dot((u * u).astype(BF16), wdn_ref[sl, :])


def _outffn(x2d, yrc, yn, woa, wob, gain, wup, wdn, tm, ff_chunk):
    t, d = x2d.shape
    row = lambda w: pl.BlockSpec((tm, w), lambda i: (i, 0))
    full = lambda a: pl.BlockSpec(a.shape, lambda i: (0,) * a.ndim)
    return pl.pallas_call(
        functools.partial(_outffn_kernel, ff_chunk=ff_chunk),
        grid=(t // tm,),
        in_specs=[row(d), row(yrc.shape[1]), row(yn.shape[1])] + [full(a) for a in (woa, wob, gain, wup, wdn)],
        out_specs=row(d),
        out_shape=jax.ShapeDtypeStruct((t, d), F32),
        compiler_params=pltpu.CompilerParams(dimension_semantics=("arbitrary",), vmem_limit_bytes=VMEM_LIMIT),
        name="outffn",
    )(x2d, yrc, yn, woa, wob, gain, wup, wdn)


def _block_diag_ones(n, block, value):
    idx = np.arange(n) // block
    return jnp.asarray((idx[:, None] == idx[None, :]).astype(np.float32) * value)


def _retention_tables(s):
    h, d, c = RET_HEADS, HEAD_DIM, RET_CHUNK
    pos = jnp.arange(s, dtype=F32)
    inv_freq = 1.0 / (XPOS_BASE ** jnp.linspace(0.0, 1.0, d // 2, dtype=F32))
    ang = jnp.repeat(pos[:, None] * inv_freq[None, :], 2, axis=-1)
    sign = jnp.tile(jnp.asarray([-1.0, 1.0], F32), d // 2)
    cos = jnp.tile(jnp.cos(ang), (1, h))
    sin = jnp.tile(jnp.sin(ang) * sign[None, :], (1, h))
    log_gamma = jnp.log(1.0 - 2.0 ** (-5.0 - jnp.arange(h, dtype=F32)))
    idx = jnp.arange(c, dtype=F32)
    rel = idx[:, None] - idx[None, :]
    dmask = jnp.where(rel >= 0, jnp.exp(log_gamma[:, None, None] * jnp.maximum(rel, 0.0)), 0.0)
    kdec = jnp.repeat(jnp.exp(log_gamma[:, None] * (c - 1.0 - idx)).T, d, axis=1)
    qdec = jnp.repeat(jnp.exp(log_gamma[:, None] * (idx + 1.0)).T, d, axis=1)
    bdm = _block_diag_ones(h * d, d, 1.0)
    sdec = bdm * jnp.repeat(jnp.exp(log_gamma * c), d)[:, None]
    return cos, sin, dmask, qdec, kdec, sdec, bdm


def _nsa_tables(s):
    n_rows = s // CMP_STRIDE
    n_cmp = (s - CMP_LEN) // CMP_STRIDE + 1
    n_slc = s // SLC_LEN
    cs = np.arange(n_cmp)[:, None] * CMP_STRIDE
    ss = np.arange(n_slc)[None, :] * SLC_LEN
    overlap = np.clip(np.minimum(cs + CMP_LEN, ss + SLC_LEN) - np.maximum(cs, ss), 0, None).astype(np.float32) / CMP_LEN
    ovt = np.zeros((n_slc, n_rows), np.float32)
    ovt[:, :n_cmp] = overlap.T
    eye = np.zeros((n_slc, LANES), np.float32)
    eye[np.arange(n_slc), np.arange(n_slc)] = 1.0
    n_chunks = s // SLC_CHUNK
    key = np.arange(SLC_CHUNK)
    e3 = np.zeros((n_chunks, LANES, SLC_CHUNK), np.float32)
    for c in range(n_chunks):
        e3[c, (c * SLC_CHUNK + key) // SLC_LEN, key] = 1.0
    dupk = np.zeros((NSA_GROUPS, KV_WIDTH, LANES), np.float32)
    dupv = np.zeros((NSA_GROUPS, KV_WIDTH, NSA_HPG * HEAD_DIM), np.float32)
    e = np.arange(HEAD_DIM)
    for g in range(NSA_GROUPS):
        for rep in range(LANES // HEAD_DIM):
            dupk[g, g * HEAD_DIM + e, rep * HEAD_DIM + e] = 1.0
        for rep in range(NSA_HPG):
            dupv[g, g * HEAD_DIM + e, rep * HEAD_DIM + e] = 1.0
    gw = NSA_HPG * HEAD_DIM
    eg = np.zeros((LANES, 3 * NSA_GROUPS * gw), np.float32)
    for br in range(3):
        for g in range(NSA_GROUPS):
            for h in range(NSA_HPG):
                c0 = (br * NSA_GROUPS + g) * gw + h * HEAD_DIM
                eg[(NSA_HPG * g + h) * 3 + br, c0:c0 + HEAD_DIM] = 1.0
    return tuple(jnp.asarray(a, BF16) for a in (ovt, eye, e3, dupk, dupv, eg))


def _cmp_weights(pe, w1, w2):
    half = CMP_LEN // 2
    eye = jnp.eye(NSA_GROUPS, dtype=F32)
    w1r = w1.reshape(2, half, HEAD_DIM, HEAD_DIM)
    w1big = jnp.einsum('arce,gh->argche', w1r, eye).reshape(2, half * KV_WIDTH, KV_WIDTH)
    pe2 = jnp.broadcast_to(pe.reshape(2, half, 1, HEAD_DIM), (2, half, NSA_GROUPS, HEAD_DIM)).reshape(2, half * KV_WIDTH)
    w2big = jnp.kron(eye, w2)
    return pe2, w1big.astype(BF16), w2big.astype(BF16)


def kernel(x, norm_mix, w_in, conv_w, nsa_q_norm, nsa_k_norm, cmp_pe_k, cmp_w1_k, cmp_w2_k,
           cmp_pe_v, cmp_w1_v, cmp_w2_v, w_out, norm_mlp, w_up, w_down):
    b, s, d = x.shape
    t = b * s
    depth = w_in.shape[0]
    tm = 512
    ret_tabs = _retention_tables(s)
    nsa_tabs = _nsa_tables(s)
    bd64 = _block_diag_ones(LANES, HEAD_DIM, 1.0 / HEAD_DIM).astype(BF16)
    bdn = _block_diag_ones(RET_WIDTH, HEAD_DIM, 1.0 / HEAD_DIM).astype(BF16)
    c_rc = 4 * RET_WIDTH + 3 * CONV_WIDTH
    c_q = c_rc + NSA_WIDTH
    c_kc = c_q + KV_WIDTH
    c_vc = c_kc + KV_WIDTH
    c_kv = c_vc + 4 * KV_WIDTH
    n_gate = 3 * NSA_HEADS
    rows16 = s // CMP_STRIDE

    xf = x.reshape(t, d)
    for l in range(depth):
        w = w_in[l].astype(BF16)
        wg = jnp.pad(w[:, c_kv:c_kv + n_gate], ((0, 0), (0, LANES - n_gate)))
        qgain = jnp.tile(nsa_q_norm[l], NSA_HEADS)[None, :LANES]
        kgain = jnp.tile(nsa_k_norm[l], NSA_GROUPS)[None, :]
        z_rc, qn, zkc, zvc, kv4, glog = _inproj(
            xf, norm_mix[l][None, :], w[:, :c_rc], w[:, c_rc:c_q], w[:, c_q:c_kc], w[:, c_kc:c_vc], w[:, c_vc:c_kv], wg,
            qgain, kgain, bd64, tm)
        y_rc = _retconv(z_rc.reshape(b, s, c_rc), *ret_tabs, bdn, conv_w[l])
        pek, w1k, w2k = _cmp_weights(cmp_pe_k[l], cmp_w1_k[l], cmp_w2_k[l])
        pev, w1v, w2v = _cmp_weights(cmp_pe_v[l], cmp_w1_v[l], cmp_w2_v[l])
        kcmp, vcmp = _cmp_prep(zkc.reshape(b, rows16, CMP_STRIDE * KV_WIDTH), zvc.reshape(b, rows16, CMP_STRIDE * KV_WIDTH),
                               pek, pev, w1k, w1v, w2k, w2v, kgain, bd64)
        y_nsa = _nsa(qn.reshape(b, s, NSA_WIDTH), kcmp, vcmp, kv4.reshape(b, s, 4 * KV_WIDTH),
                     glog.reshape(b, s, LANES), *nsa_tabs)
        wo = w_out[l].astype(BF16)
        n_rc = RET_WIDTH + CONV_WIDTH
        xf = _outffn(xf, y_rc.reshape(t, n_rc), y_nsa.reshape(t, NSA_WIDTH), wo[:n_rc], wo[n_rc:],
                     norm_mlp[l][None, :], w_up[l].astype(BF16), w_down[l].astype(BF16), tm, 1024)
    return xf.reshape(b, s, d)
```

```python
import functools

import numpy as np
import jax
import jax.numpy as jnp
from jax import lax
from jax.experimental import pallas as pl
from jax.experimental.pallas import tpu as pltpu

F32 = jnp.float32
BF16 = jnp.bfloat16

HEAD_DIM = 64
LANES = 128
RET_CHUNK = 128
RET_UNROLL = 4
RET_HEADS = 4
RET_WIDTH = RET_HEADS * HEAD_DIM
CONV_WIDTH = 256
NSA_HEADS = 8
NSA_GROUPS = 2
NSA_HPG = NSA_HEADS // NSA_GROUPS
NSA_WIDTH = NSA_HEADS * HEAD_DIM
KV_WIDTH = NSA_GROUPS * HEAD_DIM
CMP_LEN = 32
CMP_STRIDE = 16
SLC_LEN = 64
SLC_TOPK = 8
WIN_LEN = 256
Q_TILE = 128
SLC_CHUNK = 512
ROW_TILE = 512
FF_CHUNK = 1024
XPOS_BASE = 10000.0
EPS = 1e-6
NEG_INF = -1e30
FORCE_BONUS = 1e4
VMEM_LIMIT = 56 * 1024 * 1024


def _dot(a, b):
    return jnp.dot(a, b, preferred_element_type=F32)


def _dot_nt(a, b):
    return lax.dot_general(a, b, (((1,), (1,)), ((), ())), preferred_element_type=F32)


def _dot_tn(a, b):
    return lax.dot_general(a, b, (((0,), (0,)), ((), ())), preferred_element_type=F32)


def _split_bf16(x):
    hi = x.astype(BF16)
    lo = (x - hi.astype(F32)).astype(BF16)
    return hi, lo


def _group_rms(z, bd):
    return z * lax.rsqrt(_dot((z * z).astype(BF16), bd) + EPS)


def _inproj_kernel(x_ref, gain_ref, wrc_ref, wq_ref, wkc_ref, wvc_ref, wkv_ref, wg_ref,
                   qgain_ref, kgain_ref, bd_ref,
                   orc_ref, oq_ref, okc_ref, ovc_ref, okv_ref, og_ref):
    x = x_ref[...]
    h = x * lax.rsqrt(jnp.mean(x * x, axis=-1, keepdims=True) + EPS) * gain_ref[...]
    h = h.astype(BF16)
    bd = bd_ref[...]
    wide = bd.shape[0]
    zq = _dot(h, wq_ref[...])
    qscale = qgain_ref[...] * (HEAD_DIM ** -0.5)
    for c in range(NSA_WIDTH // wide):
        sl = slice(c * wide, (c + 1) * wide)
        oq_ref[:, sl] = (_group_rms(zq[:, sl], bd) * qscale).astype(BF16)
    okc_ref[...] = _dot(h, wkc_ref[...]).astype(BF16)
    ovc_ref[...] = _dot(h, wvc_ref[...]).astype(BF16)
    zkv = _dot(h, wkv_ref[...])
    kw = KV_WIDTH
    kn = _group_rms(jnp.concatenate([zkv[:, 0:kw], zkv[:, 2 * kw:3 * kw]], axis=1), bd) * kgain_ref[...]
    okv_ref[:, 0:kw] = kn[:, 0:kw].astype(BF16)
    okv_ref[:, kw:2 * kw] = zkv[:, kw:2 * kw].astype(BF16)
    okv_ref[:, 2 * kw:3 * kw] = kn[:, kw:2 * kw].astype(BF16)
    okv_ref[:, 3 * kw:4 * kw] = zkv[:, 3 * kw:4 * kw].astype(BF16)
    og_ref[...] = _dot(h, wg_ref[...]).astype(BF16)
    orc_ref[...] = _dot(h, wrc_ref[...]).astype(BF16)


def _inproj(x2d, gain, wrc, wq, wkc, wvc, wkv, wg, qgain, kgain, bd):
    t, d = x2d.shape
    tm = ROW_TILE
    row = lambda w: pl.BlockSpec((tm, w), lambda i: (i, 0))
    full = lambda a: pl.BlockSpec(a.shape, lambda i: (0,) * a.ndim)
    widths = (wrc.shape[1], wq.shape[1], wkc.shape[1], wvc.shape[1], wkv.shape[1], wg.shape[1])
    return pl.pallas_call(
        _inproj_kernel,
        grid=(t // tm,),
        in_specs=[row(d)] + [full(a) for a in (gain, wrc, wq, wkc, wvc, wkv, wg, qgain, kgain, bd)],
        out_specs=[row(w) for w in widths],
        out_shape=[jax.ShapeDtypeStruct((t, w), BF16) for w in widths],
        compiler_params=pltpu.CompilerParams(dimension_semantics=("arbitrary",), vmem_limit_bytes=VMEM_LIMIT),
        name="inproj",
    )(x2d, gain, wrc, wq, wkc, wvc, wkv, wg, qgain, kgain, bd)


def _retconv_kernel(z_ref, cos_ref, sin_ref, dmask_ref, qdec_ref, kdec_ref, sdec_ref, bdm_ref, bdn_ref, convw_ref,
                    out_ref, state_ref):
    s_len = z_ref.shape[1]
    w = RET_WIDTH
    lane = lax.broadcasted_iota(jnp.int32, (RET_CHUNK, w), 1)
    even = lane % 2 == 0
    head = lane // HEAD_DIM

    def rotate(xb, cos, sin):
        x = xb.astype(F32)
        swapped = jnp.where(even, pltpu.roll(x, w - 1, 1), pltpu.roll(x, 1, 1))
        return x * cos + swapped * sin

    state_ref[...] = jnp.zeros_like(state_ref)

    def chunk(c, carry):
        r0 = pl.multiple_of(c * RET_CHUNK, RET_CHUNK)
        rows = pl.ds(r0, RET_CHUNK)
        cos = cos_ref[rows, :]
        sin = sin_ref[rows, :]
        q = rotate(z_ref[0, rows, 0:w], cos, sin)
        k = rotate(z_ref[0, rows, w:2 * w], cos, sin) * (HEAD_DIM ** -0.5)
        vb = z_ref[0, rows, 2 * w:3 * w]
        gate = z_ref[0, rows, 3 * w:4 * w].astype(F32)
        qb = q.astype(BF16)
        kb = k.astype(BF16)
        zero = jnp.zeros_like(qb)
        o = jnp.zeros((RET_CHUNK, w), F32)
        for h in range(RET_HEADS):
            hm = head == h
            scores = _dot_nt(jnp.where(hm, qb, zero), kb) * dmask_ref[h]
            o = o + _dot(scores.astype(BF16), jnp.where(hm, vb, zero))
        state = state_ref[...]
        o = o + _dot((q * qdec_ref[...]).astype(BF16), state.astype(BF16))
        kv = _dot_tn((k * kdec_ref[...]).astype(BF16), vb)
        state_ref[...] = state * sdec_ref[...] + kv * bdm_ref[...]
        o = _group_rms(o, bdn_ref[...])
        y = gate * jax.nn.sigmoid(gate) * o
        out_ref[0, rows, 0:w] = y.astype(BF16)
        return carry

    lax.fori_loop(0, s_len // RET_CHUNK, chunk, 0, unroll=RET_UNROLL)

    c0 = 4 * w
    cw = CONV_WIDTH
    b_gate = z_ref[0, :, c0:c0 + cw].astype(F32)
    u = z_ref[0, :, c0 + cw:c0 + 2 * cw].astype(F32) * z_ref[0, :, c0 + 2 * cw:c0 + 3 * cw].astype(F32)
    t_idx = lax.broadcasted_iota(jnp.int32, (s_len, cw), 0)
    u1 = jnp.where(t_idx >= 1, pltpu.roll(u, 1, 0), 0.0)
    u2 = jnp.where(t_idx >= 2, pltpu.roll(u, 2, 0), 0.0)
    y = b_gate * (convw_ref[0:1, :] * u2 + convw_ref[1:2, :] * u1 + convw_ref[2:3, :] * u)
    out_ref[0, :, w:w + cw] = y.astype(BF16)


def _retconv(z_rc, cos, sin, dmask, qdec, kdec, sdec, bdm, bdn, convw):
    b, s, wz = z_rc.shape
    full = lambda a: pl.BlockSpec(a.shape, lambda i: (0,) * a.ndim)
    return pl.pallas_call(
        _retconv_kernel,
        grid=(b,),
        in_specs=[pl.BlockSpec((1, s, wz), lambda i: (i, 0, 0))]
        + [full(a) for a in (cos, sin, dmask, qdec, kdec, sdec, bdm, bdn, convw)],
        out_specs=pl.BlockSpec((1, s, RET_WIDTH + CONV_WIDTH), lambda i: (i, 0, 0)),
        out_shape=jax.ShapeDtypeStruct((b, s, RET_WIDTH + CONV_WIDTH), BF16),
        scratch_shapes=[pltpu.VMEM((RET_WIDTH, RET_WIDTH), F32)],
        compiler_params=pltpu.CompilerParams(dimension_semantics=("arbitrary",), vmem_limit_bytes=VMEM_LIMIT),
        name="retconv",
    )(z_rc, cos, sin, dmask, qdec, kdec, sdec, bdm, bdn, convw)


def _cmp_kernel(xk_ref, xv_ref, pek_ref, pev_ref, w1k_ref, w1v_ref, w2k_ref, w2v_ref, kgain_ref, bd_ref,
                ok_ref, ov_ref):
    nrow = xk_ref.shape[1]
    row = lax.broadcasted_iota(jnp.int32, (nrow, KV_WIDTH), 0)

    def compress(x_ref, pe_ref, w1_ref, w2_ref):
        x = x_ref[0].astype(F32)
        first = _dot((x + pe_ref[0:1, :]).astype(BF16), w1_ref[0])
        second = _dot((x + pe_ref[1:2, :]).astype(BF16), w1_ref[1])
        pre = first + pltpu.roll(second, nrow - 1, 0)
        return _dot(jax.nn.silu(pre).astype(BF16), w2_ref[...])

    valid = row < nrow - 1
    kc = _group_rms(compress(xk_ref, pek_ref, w1k_ref, w2k_ref), bd_ref[...]) * kgain_ref[...]
    ok_ref[0] = jnp.where(valid, kc, 0.0).astype(BF16)
    vc = compress(xv_ref, pev_ref, w1v_ref, w2v_ref)
    ov_ref[0] = jnp.where(valid, vc, 0.0).astype(BF16)


def _cmp_prep(xk16, xv16, pek, pev, w1k, w1v, w2k, w2v, kgain, bd):
    b, nrow, wide = xk16.shape
    full = lambda a: pl.BlockSpec(a.shape, lambda i: (0,) * a.ndim)
    blk = pl.BlockSpec((1, nrow, wide), lambda i: (i, 0, 0))
    oblk = pl.BlockSpec((1, nrow, KV_WIDTH), lambda i: (i, 0, 0))
    return pl.pallas_call(
        _cmp_kernel,
        grid=(b,),
        in_specs=[blk, blk] + [full(a) for a in (pek, pev, w1k, w1v, w2k, w2v, kgain, bd)],
        out_specs=[oblk, oblk],
        out_shape=[jax.ShapeDtypeStruct((b, nrow, KV_WIDTH), BF16)] * 2,
        compiler_params=pltpu.CompilerParams(dimension_semantics=("arbitrary",), vmem_limit_bytes=VMEM_LIMIT),
        name="cmp_prep",
    )(xk16, xv16, pek, pev, w1k, w1v, w2k, w2v, kgain, bd)


def _nsa_kernel(q_ref, kc_ref, vc_ref, kv_ref, gl_ref, ovt_ref, swap_ref,
                out_ref, vtc_ref, vts_ref, vtw_ref, sel_ref, sc_ref, m_ref, acc_ref):
    i = pl.program_id(1)
    n_slc = ovt_ref.shape[0]
    tq = Q_TILE
    kc = SLC_CHUNK
    nb = kc // SLC_LEN
    s_len = kv_ref.shape[1]
    cols = NSA_HPG * tq
    groups = range(NSA_GROUPS)

    def transposed_v(v, g):
        vt = v.astype(F32).T
        other = lax.broadcasted_iota(jnp.int32, vt.shape, 0) // HEAD_DIM != g
        return jnp.where(other, 1.0, vt).astype(BF16)

    @pl.when(i == 0)
    def _():
        for g in groups:
            vtc_ref[g] = transposed_v(vc_ref[0], g)
            for c in range(s_len // kc):
                vts_ref[g, c] = transposed_v(kv_ref[0, c * kc:(c + 1) * kc, KV_WIDTH:2 * KV_WIDTH], g)
            for c in range(s_len // tq):
                vtw_ref[g, c] = transposed_v(kv_ref[0, c * tq:(c + 1) * tq, 3 * KV_WIDTH:4 * KV_WIDTH], g)

    def tile4(a):
        return jnp.concatenate([a] * NSA_HPG, axis=1)

    def colmax(a):
        return jnp.max(a, axis=0, keepdims=True)

    q = q_ref[0]
    lane = lax.broadcasted_iota(jnp.int32, (tq, LANES), 1)
    t_lane = i * tq + lane
    row = lax.broadcasted_iota(jnp.int32, (tq, LANES), 0)
    gates_t = jax.nn.sigmoid(gl_ref[0].astype(F32)).T

    q4s = []
    for g in groups:
        qs = []
        for h in range(NSA_HPG):
            hd = NSA_HPG * g + h
            qp = q[:, (hd // 2) * LANES:(hd // 2 + 1) * LANES]
            if hd % 2 != g:
                qp = _dot(qp, swap_ref[...]).astype(BF16)
            keep = (lane >= HEAD_DIM) if g else (lane < HEAD_DIM)
            qs.append(jnp.where(keep, qp, jnp.zeros_like(qp)))
        q4s.append(jnp.concatenate(qs, axis=0))

    nwt = WIN_LEN // tq + 1
    wtile = jnp.maximum(i - WIN_LEN // tq, 0)
    kpos = wtile * tq + lax.broadcasted_iota(jnp.int32, (nwt * tq, LANES), 0)
    t_win = i * tq + lax.broadcasted_iota(jnp.int32, (nwt * tq, LANES), 1)
    wbias = tile4(jnp.where(kpos <= t_win, jnp.where(kpos > t_win - WIN_LEN, 0.0, NEG_INF), NEG_INF))
    wrows = pl.ds(pl.multiple_of(wtile * tq, tq), nwt * tq)
    cmp_bias = tile4(jnp.where(row * CMP_STRIDE + (CMP_LEN - 1) <= t_lane, 0.0, NEG_INF))
    cmp_any = tile4(jnp.where(t_lane[0:1, :] >= CMP_LEN - 1, 1.0, 0.0))

    o_cmps, o_wins = [], []
    for g in groups:
        rs = slice(g * HEAD_DIM, (g + 1) * HEAD_DIM)
        ls = (1 - g) * HEAD_DIM

        sw = _dot_nt(kv_ref[0, wrows, 2 * KV_WIDTH:3 * KV_WIDTH], q4s[g]) + wbias
        vtw = jnp.concatenate([vtw_ref[g, wtile + j] for j in range(nwt)], axis=1)
        ow = _dot(vtw, jnp.exp((sw - colmax(sw)).astype(BF16)))
        o_wins.append(ow[rs] * (1.0 / ow[ls:ls + 1]))

        sc = _dot_nt(kc_ref[0], q4s[g]) + cmp_bias
        e = jnp.exp(sc - colmax(sc))
        scale = cmp_any * (1.0 / jnp.maximum(jnp.sum(e, axis=0, keepdims=True), 1e-30))
        p = e * scale
        o_cmps.append(_dot(vtc_ref[g], p.astype(BF16))[rs])
        psum = p[:, 0:tq]
        for h in range(1, NSA_HPG):
            psum = psum + p[:, h * tq:(h + 1) * tq]
        p_hi, p_lo = _split_bf16(psum)
        imp = _dot(ovt_ref[...], p_hi) + _dot(ovt_ref[...], p_lo)

        j_idx = lax.broadcasted_iota(jnp.int32, (n_slc, tq), 0)
        t_blk = (i * tq + lax.broadcasted_iota(jnp.int32, (n_slc, tq), 1)) // SLC_LEN
        forced = (j_idx == 0) | (j_idx == t_blk) | (j_idx == t_blk - 1)
        score = jnp.where(j_idx <= t_blk, imp + jnp.where(forced, FORCE_BONUS, 0.0), -1.0)
        blocks = [score[8 * v:8 * v + 8] for v in range(n_slc // 8)]
        jrow = lax.broadcasted_iota(jnp.int32, (8, tq), 0)
        ranks = [jnp.zeros((8, tq), F32) for _ in blocks]
        for k in range(n_slc):
            sk = score[k:k + 1, :]
            for v, sb in enumerate(blocks):
                if 8 * v > k:
                    ahead = sk >= sb
                elif 8 * v + 7 < k:
                    ahead = sk > sb
                else:
                    ahead = jnp.where(jrow + 8 * v > k, jnp.where(sk >= sb, 1.0, 0.0), jnp.where(sk > sb, 1.0, 0.0)) > 0.5
                ranks[v] = ranks[v] + jnp.where(ahead, 1.0, 0.0)
        for v, r in enumerate(ranks):
            sel = jnp.where(r < float(SLC_TOPK), 0.0, NEG_INF)
            sel_ref[g, 8 * v:8 * v + 8] = jnp.broadcast_to(sel[:, None, :], (8, 8, tq))

    n_chunks = (i * tq) // kc + 1
    m_ref[...] = jnp.full(m_ref.shape, NEG_INF, F32)
    acc_ref[...] = jnp.zeros_like(acc_ref)
    krow = lax.broadcasted_iota(jnp.int32, (kc, LANES), 0)
    tpos = i * tq + lax.broadcasted_iota(jnp.int32, (kc, LANES), 1)

    def pass1(c, carry):
        krows = kv_ref[0, pl.ds(pl.multiple_of(c * kc, kc), kc), 0:KV_WIDTH]
        causal = (c * kc + krow) <= tpos
        for g in groups:
            chosen = jnp.concatenate(
                [jnp.broadcast_to(sel_ref[g, c * nb + jb][None], (SLC_LEN // 8, 8, tq)).reshape(SLC_LEN, tq)
                 for jb in range(nb)], axis=0)
            sc = _dot_nt(krows, q4s[g]) + tile4(jnp.where(causal, chosen, NEG_INF))
            sc_ref[g, c] = sc
            m_ref[g] = jnp.maximum(m_ref[g], jnp.max(sc.reshape(kc // 8, 8, cols), axis=0))
        return carry

    lax.fori_loop(0, n_chunks, pass1, 0)
    m_rows = [colmax(m_ref[g]) for g in groups]

    def pass2(c, carry):
        for g in groups:
            acc_ref[g] += _dot(vts_ref[g, c], jnp.exp((sc_ref[g, c] - m_rows[g]).astype(BF16)))
        return carry

    lax.fori_loop(0, n_chunks, pass2, 0)

    ys = []
    for g in groups:
        rs = slice(g * HEAD_DIM, (g + 1) * HEAD_DIM)
        ls = (1 - g) * HEAD_DIM
        acc = acc_ref[g]
        o_slc = acc[rs] * (1.0 / acc[ls:ls + 1])
        for h in range(NSA_HPG):
            r = (NSA_HPG * g + h) * 3
            blk = slice(h * tq, (h + 1) * tq)
            ys.append(gates_t[r:r + 1] * o_cmps[g][:, blk] + gates_t[r + 1:r + 2] * o_slc[:, blk]
                      + gates_t[r + 2:r + 3] * o_wins[g][:, blk])
    out_ref[0] = jnp.concatenate(ys, axis=0).T.astype(BF16)


def _nsa(qn, kcmp, vcmp, kv4, glog, ovt, swap):
    b, s, _ = qn.shape
    nq = s // Q_TILE
    ncmp = kcmp.shape[1]
    n_slc = ovt.shape[0]
    cols = NSA_HPG * Q_TILE
    full = lambda a: pl.BlockSpec(a.shape, lambda bi, i: (0,) * a.ndim)
    return pl.pallas_call(
        _nsa_kernel,
        grid=(b, nq),
        in_specs=[pl.BlockSpec((1, Q_TILE, NSA_WIDTH), lambda bi, i: (bi, i, 0)),
                  pl.BlockSpec((1, ncmp, KV_WIDTH), lambda bi, i: (bi, 0, 0)),
                  pl.BlockSpec((1, ncmp, KV_WIDTH), lambda bi, i: (bi, 0, 0)),
                  pl.BlockSpec((1, s, 4 * KV_WIDTH), lambda bi, i: (bi, 0, 0)),
                  pl.BlockSpec((1, Q_TILE, LANES), lambda bi, i: (bi, i, 0)),
                  full(ovt), full(swap)],
        out_specs=pl.BlockSpec((1, Q_TILE, NSA_WIDTH), lambda bi, i: (bi, i, 0)),
        out_shape=jax.ShapeDtypeStruct((b, s, NSA_WIDTH), BF16),
        scratch_shapes=[pltpu.VMEM((NSA_GROUPS, KV_WIDTH, ncmp), BF16),
                        pltpu.VMEM((NSA_GROUPS, s // SLC_CHUNK, KV_WIDTH, SLC_CHUNK), BF16),
                        pltpu.VMEM((NSA_GROUPS, nq, KV_WIDTH, Q_TILE), BF16),
                        pltpu.VMEM((NSA_GROUPS, n_slc, 8, Q_TILE), F32),
                        pltpu.VMEM((NSA_GROUPS, s // SLC_CHUNK, SLC_CHUNK, cols), F32),
                        pltpu.VMEM((NSA_GROUPS, 8, cols), F32),
                        pltpu.VMEM((NSA_GROUPS, KV_WIDTH, cols), F32)],
        compiler_params=pltpu.CompilerParams(dimension_semantics=("arbitrary", "arbitrary"),
                                             vmem_limit_bytes=VMEM_LIMIT),
        name="nsa",
    )(qn, kcmp, vcmp, kv4, glog, ovt, swap)


def _outffn_kernel(x_ref, yrc_ref, yn_ref, woa_ref, wob_ref, gain_ref, wup_ref, wdn_ref, out_ref):
    x1 = x_ref[...] + _dot(yrc_ref[...], woa_ref[...]) + _dot(yn_ref[...], wob_ref[...])
    h = x1 * lax.rsqrt(jnp.mean(x1 * x1, axis=-1, keepdims=True) + EPS) * gain_ref[...]
    h = h.astype(BF16)
    out_ref[...] = x1
    for c in range(wup_ref.shape[1] // FF_CHUNK):
        sl = slice(c * FF_CHUNK, (c + 1) * FF_CHUNK)
        u = jnp.maximum(_dot(h, wup_ref[:, sl]), 0.0)
        out_ref[...] += _dot((u * u).astype(BF16), wdn_ref[sl, :])


def _outffn(x2d, yrc, yn, woa, wob, gain, wup, wdn):
    t, d = x2d.shape
    tm = ROW_TILE
    row = lambda w: pl.BlockSpec((tm, w), lambda i: (i, 0))
    full = lambda a: pl.BlockSpec(a.shape, lambda i: (0,) * a.ndim)
    return pl.pallas_call(
        _outffn_kernel,
        grid=(t // tm,),
        in_specs=[row(d), row(yrc.shape[1]), row(yn.shape[1])] + [full(a) for a in (woa, wob, gain, wup, wdn)],
        out_specs=row(d),
        out_shape=jax.ShapeDtypeStruct((t, d), F32),
        compiler_params=pltpu.CompilerParams(dimension_semantics=("arbitrary",), vmem_limit_bytes=VMEM_LIMIT),
        name="outffn",
    )(x2d, yrc, yn, woa, wob, gain, wup, wdn)


def _block_diag_ones(n, block, value):
    idx = np.arange(n) // block
    return jnp.asarray((idx[:, None] == idx[None, :]).astype(np.float32) * value)


def _retention_tables(s):
    h, d, c = RET_HEADS, HEAD_DIM, RET_CHUNK
    pos = jnp.arange(s, dtype=F32)
    inv_freq = 1.0 / (XPOS_BASE ** jnp.linspace(0.0, 1.0, d // 2, dtype=F32))
    ang = jnp.repeat(pos[:, None] * inv_freq[None, :], 2, axis=-1)
    sign = jnp.tile(jnp.asarray([-1.0, 1.0], F32), d // 2)
    cos = jnp.tile(jnp.cos(ang), (1, h))
    sin = jnp.tile(jnp.sin(ang) * sign[None, :], (1, h))
    log_gamma = jnp.log(1.0 - 2.0 ** (-5.0 - jnp.arange(h, dtype=F32)))
    idx = jnp.arange(c, dtype=F32)
    rel = idx[:, None] - idx[None, :]
    dmask = jnp.where(rel >= 0, jnp.exp(log_gamma[:, None, None] * jnp.maximum(rel, 0.0)), 0.0)
    kdec = jnp.repeat(jnp.exp(log_gamma[:, None] * (c - 1.0 - idx)).T, d, axis=1)
    qdec = jnp.repeat(jnp.exp(log_gamma[:, None] * (idx + 1.0)).T, d, axis=1)
    bdm = _block_diag_ones(h * d, d, 1.0)
    sdec = bdm * jnp.repeat(jnp.exp(log_gamma * c), d)[:, None]
    return cos, sin, dmask, qdec, kdec, sdec, bdm


def _nsa_tables(s):
    n_rows = s // CMP_STRIDE
    n_cmp = (s - CMP_LEN) // CMP_STRIDE + 1
    n_slc = s // SLC_LEN
    cs = np.arange(n_cmp)[:, None] * CMP_STRIDE
    ss = np.arange(n_slc)[None, :] * SLC_LEN
    overlap = np.clip(np.minimum(cs + CMP_LEN, ss + SLC_LEN) - np.maximum(cs, ss), 0, None).astype(np.float32) / CMP_LEN
    ovt = np.zeros((n_slc, n_rows), np.float32)
    ovt[:, :n_cmp] = overlap.T
    swap = np.zeros((LANES, LANES), np.float32)
    swap[np.arange(LANES), (np.arange(LANES) + HEAD_DIM) % LANES] = 1.0
    return jnp.asarray(ovt, BF16), jnp.asarray(swap, BF16)


def _cmp_weights(pe, w1, w2):
    half = CMP_LEN // 2
    eye = jnp.eye(NSA_GROUPS, dtype=F32)
    w1r = w1.reshape(2, half, HEAD_DIM, HEAD_DIM)
    w1big = jnp.einsum('arce,gh->argche', w1r, eye).reshape(2, half * KV_WIDTH, KV_WIDTH)
    pe2 = jnp.broadcast_to(pe.reshape(2, half, 1, HEAD_DIM), (2, half, NSA_GROUPS, HEAD_DIM)).reshape(2, half * KV_WIDTH)
    w2big = jnp.kron(eye, w2)
    return pe2, w1big.astype(BF16), w2big.astype(BF16)


def kernel(x, norm_mix, w_in, conv_w, nsa_q_norm, nsa_k_norm, cmp_pe_k, cmp_w1_k, cmp_w2_k,
           cmp_pe_v, cmp_w1_v, cmp_w2_v, w_out, norm_mlp, w_up, w_down):
    b, s, d = x.shape
    t = b * s
    depth = w_in.shape[0]
    ret_tabs = _retention_tables(s)
    nsa_tabs = _nsa_tables(s)
    bd128 = _block_diag_ones(KV_WIDTH, HEAD_DIM, 1.0 / HEAD_DIM).astype(BF16)
    bd256 = _block_diag_ones(RET_WIDTH, HEAD_DIM, 1.0 / HEAD_DIM).astype(BF16)
    c_rc = 4 * RET_WIDTH + 3 * CONV_WIDTH
    c_q = c_rc + NSA_WIDTH
    c_kc = c_q + KV_WIDTH
    c_vc = c_kc + KV_WIDTH
    c_kv = c_vc + 4 * KV_WIDTH
    n_gate = 3 * NSA_HEADS
    rows16 = s // CMP_STRIDE
    n_rc = RET_WIDTH + CONV_WIDTH

    xf = x.reshape(t, d)
    for l in range(depth):
        w = w_in[l].astype(BF16)
        wg = jnp.pad(w[:, c_kv:c_kv + n_gate], ((0, 0), (0, LANES - n_gate)))
        qgain = jnp.tile(nsa_q_norm[l], RET_WIDTH // HEAD_DIM)[None, :]
        kgain = jnp.tile(nsa_k_norm[l], NSA_GROUPS)[None, :]
        z_rc, qn, zkc, zvc, kv4, glog = _inproj(
            xf, norm_mix[l][None, :], w[:, :c_rc], w[:, c_rc:c_q], w[:, c_q:c_kc], w[:, c_kc:c_vc], w[:, c_vc:c_kv], wg,
            qgain, jnp.tile(kgain, (1, 2)), bd256)
        y_rc = _retconv(z_rc.reshape(b, s, c_rc), *ret_tabs, bd256, conv_w[l])
        pek, w1k, w2k = _cmp_weights(cmp_pe_k[l], cmp_w1_k[l], cmp_w2_k[l])
        pev, w1v, w2v = _cmp_weights(cmp_pe_v[l], cmp_w1_v[l], cmp_w2_v[l])
        kcmp, vcmp = _cmp_prep(zkc.reshape(b, rows16, CMP_STRIDE * KV_WIDTH), zvc.reshape(b, rows16, CMP_STRIDE * KV_WIDTH),
                               pek, pev, w1k, w1v, w2k, w2v, kgain, bd128)
        y_nsa = _nsa(qn.reshape(b, s, NSA_WIDTH), kcmp, vcmp, kv4.reshape(b, s, 4 * KV_WIDTH),
                     glog.reshape(b, s, LANES), *nsa_tabs)
        wo = w_out[l].astype(BF16)
        xf = _outffn(xf, y_rc.reshape(t, n_rc), y_nsa.reshape(t, NSA_WIDTH), wo[:n_rc], wo[n_rc:],
                     norm_mlp[l][None, :], w_up[l].astype(BF16), w_down[l].astype(BF16))
    return xf.reshape(b, s, d)
```

```python
import functools

import numpy as np
import jax
import jax.numpy as jnp
from jax import lax
from jax.experimental import pallas as pl
from jax.experimental.pallas import tpu as pltpu

F32 = jnp.float32
BF16 = jnp.bfloat16

HEAD_DIM = 64
LANES = 128
RET_CHUNK = 128
RET_UNROLL = 4
RET_HEADS = 4
RET_WIDTH = RET_HEADS * HEAD_DIM
CONV_WIDTH = 256
NSA_HEADS = 8
NSA_GROUPS = 2
NSA_HPG = NSA_HEADS // NSA_GROUPS
NSA_WIDTH = NSA_HEADS * HEAD_DIM
KV_WIDTH = NSA_GROUPS * HEAD_DIM
CMP_LEN = 32
CMP_STRIDE = 16
SLC_LEN = 64
SLC_TOPK = 8
WIN_LEN = 256
Q_TILE = 256
SLC_CHUNK = 512
ROW_TILE = 512
FF_CHUNK = 1024
XPOS_BASE = 10000.0
EPS = 1e-6
NEG_INF = -1e30
FORCE_BONUS = 1e4
VMEM_LIMIT = 56 * 1024 * 1024


def _dot(a, b):
    return jnp.dot(a, b, preferred_element_type=F32)


def _dot_nt(a, b):
    return lax.dot_general(a, b, (((1,), (1,)), ((), ())), preferred_element_type=F32)


def _dot_tn(a, b):
    return lax.dot_general(a, b, (((0,), (0,)), ((), ())), preferred_element_type=F32)


def _split_bf16(x):
    hi = x.astype(BF16)
    lo = (x - hi.astype(F32)).astype(BF16)
    return hi, lo


def _group_rms(z, bd):
    return z * lax.rsqrt(_dot((z * z).astype(BF16), bd) + EPS)


def _inproj_kernel(x_ref, gain_ref, wrc_ref, wq_ref, wkc_ref, wvc_ref, wkv_ref, wg_ref,
                   qgain_ref, kgain_ref, bd_ref,
                   orc_ref, oq_ref, okc_ref, ovc_ref, okv_ref, og_ref):
    x = x_ref[...]
    h = x * lax.rsqrt(jnp.mean(x * x, axis=-1, keepdims=True) + EPS) * gain_ref[...]
    h = h.astype(BF16)
    bd = bd_ref[...]
    wide = bd.shape[0]
    zq = _dot(h, wq_ref[...])
    qscale = qgain_ref[...] * (HEAD_DIM ** -0.5)
    lane_half = lax.broadcasted_iota(jnp.int32, (x.shape[0], LANES), 1) // HEAD_DIM
    for g in range(NSA_WIDTH // wide):
        qn = _group_rms(zq[:, g * wide:(g + 1) * wide], bd) * qscale
        for hh in range(NSA_HPG):
            blk = qn[:, (hh // 2) * LANES:(hh // 2 + 1) * LANES]
            if hh % 2 != g:
                blk = pltpu.roll(blk, HEAD_DIM, 1)
            hd = NSA_HPG * g + hh
            oq_ref[:, hd * LANES:(hd + 1) * LANES] = jnp.where(lane_half == g, blk, 0.0).astype(BF16)
    okc_ref[...] = _dot(h, wkc_ref[...]).astype(BF16)
    ovc_ref[...] = _dot(h, wvc_ref[...]).astype(BF16)
    zkv = _dot(h, wkv_ref[...])
    kw = KV_WIDTH
    kn = _group_rms(jnp.concatenate([zkv[:, 0:kw], zkv[:, 2 * kw:3 * kw]], axis=1), bd) * kgain_ref[...]
    okv_ref[:, 0:kw] = kn[:, 0:kw].astype(BF16)
    okv_ref[:, kw:2 * kw] = zkv[:, kw:2 * kw].astype(BF16)
    okv_ref[:, 2 * kw:3 * kw] = kn[:, kw:2 * kw].astype(BF16)
    okv_ref[:, 3 * kw:4 * kw] = zkv[:, 3 * kw:4 * kw].astype(BF16)
    og_ref[...] = _dot(h, wg_ref[...]).astype(BF16)
    orc_ref[...] = _dot(h, wrc_ref[...]).astype(BF16)


def _inproj(x2d, gain, wrc, wq, wkc, wvc, wkv, wg, qgain, kgain, bd):
    t, d = x2d.shape
    tm = ROW_TILE
    row = lambda w: pl.BlockSpec((tm, w), lambda i: (i, 0))
    full = lambda a: pl.BlockSpec(a.shape, lambda i: (0,) * a.ndim)
    widths = (wrc.shape[1], NSA_HEADS * LANES, wkc.shape[1], wvc.shape[1], wkv.shape[1], wg.shape[1])
    return pl.pallas_call(
        _inproj_kernel,
        grid=(t // tm,),
        in_specs=[row(d)] + [full(a) for a in (gain, wrc, wq, wkc, wvc, wkv, wg, qgain, kgain, bd)],
        out_specs=[row(w) for w in widths],
        out_shape=[jax.ShapeDtypeStruct((t, w), BF16) for w in widths],
        compiler_params=pltpu.CompilerParams(dimension_semantics=("arbitrary",), vmem_limit_bytes=VMEM_LIMIT),
        name="inproj",
    )(x2d, gain, wrc, wq, wkc, wvc, wkv, wg, qgain, kgain, bd)


def _retconv_kernel(z_ref, cos_ref, sin_ref, dmask_ref, qdec_ref, kdec_ref, sdec_ref, bdm_ref, bdn_ref, convw_ref,
                    out_ref, state_ref):
    s_len = z_ref.shape[1]
    w = RET_WIDTH
    lane = lax.broadcasted_iota(jnp.int32, (RET_CHUNK, w), 1)
    even = lane % 2 == 0
    head = lane // HEAD_DIM

    def rotate(xb, cos, sin):
        x = xb.astype(F32)
        swapped = jnp.where(even, pltpu.roll(x, w - 1, 1), pltpu.roll(x, 1, 1))
        return x * cos + swapped * sin

    state_ref[...] = jnp.zeros_like(state_ref)

    def chunk(c, carry):
        r0 = pl.multiple_of(c * RET_CHUNK, RET_CHUNK)
        rows = pl.ds(r0, RET_CHUNK)
        cos = cos_ref[rows, :]
        sin = sin_ref[rows, :]
        q = rotate(z_ref[0, rows, 0:w], cos, sin)
        k = rotate(z_ref[0, rows, w:2 * w], cos, sin) * (HEAD_DIM ** -0.5)
        vb = z_ref[0, rows, 2 * w:3 * w]
        gate = z_ref[0, rows, 3 * w:4 * w].astype(F32)
        qb = q.astype(BF16)
        kb = k.astype(BF16)
        zero = jnp.zeros_like(qb)
        o = jnp.zeros((RET_CHUNK, w), F32)
        for h in range(RET_HEADS):
            hm = head == h
            scores = _dot_nt(jnp.where(hm, qb, zero), kb) * dmask_ref[h]
            o = o + _dot(scores.astype(BF16), jnp.where(hm, vb, zero))
        state = state_ref[...]
        o = o + _dot((q * qdec_ref[...]).astype(BF16), state.astype(BF16))
        kv = _dot_tn((k * kdec_ref[...]).astype(BF16), vb)
        state_ref[...] = state * sdec_ref[...] + kv * bdm_ref[...]
        o = _group_rms(o, bdn_ref[...])
        y = gate * jax.nn.sigmoid(gate) * o
        out_ref[0, rows, 0:w] = y.astype(BF16)
        return carry

    lax.fori_loop(0, s_len // RET_CHUNK, chunk, 0, unroll=RET_UNROLL)

    c0 = 4 * w
    cw = CONV_WIDTH
    b_gate = z_ref[0, :, c0:c0 + cw].astype(F32)
    u = z_ref[0, :, c0 + cw:c0 + 2 * cw].astype(F32) * z_ref[0, :, c0 + 2 * cw:c0 + 3 * cw].astype(F32)
    t_idx = lax.broadcasted_iota(jnp.int32, (s_len, cw), 0)
    u1 = jnp.where(t_idx >= 1, pltpu.roll(u, 1, 0), 0.0)
    u2 = jnp.where(t_idx >= 2, pltpu.roll(u, 2, 0), 0.0)
    y = b_gate * (convw_ref[0:1, :] * u2 + convw_ref[1:2, :] * u1 + convw_ref[2:3, :] * u)
    out_ref[0, :, w:w + cw] = y.astype(BF16)


def _retconv(z_rc, cos, sin, dmask, qdec, kdec, sdec, bdm, bdn, convw):
    b, s, wz = z_rc.shape
    full = lambda a: pl.BlockSpec(a.shape, lambda i: (0,) * a.ndim)
    return pl.pallas_call(
        _retconv_kernel,
        grid=(b,),
        in_specs=[pl.BlockSpec((1, s, wz), lambda i: (i, 0, 0))]
        + [full(a) for a in (cos, sin, dmask, qdec, kdec, sdec, bdm, bdn, convw)],
        out_specs=pl.BlockSpec((1, s, RET_WIDTH + CONV_WIDTH), lambda i: (i, 0, 0)),
        out_shape=jax.ShapeDtypeStruct((b, s, RET_WIDTH + CONV_WIDTH), BF16),
        scratch_shapes=[pltpu.VMEM((RET_WIDTH, RET_WIDTH), F32)],
        compiler_params=pltpu.CompilerParams(dimension_semantics=("arbitrary",), vmem_limit_bytes=VMEM_LIMIT),
        name="retconv",
    )(z_rc, cos, sin, dmask, qdec, kdec, sdec, bdm, bdn, convw)


def _cmp_kernel(xk_ref, xv_ref, pek_ref, pev_ref, w1k_ref, w1v_ref, w2k_ref, w2v_ref, kgain_ref, bd_ref,
                ok_ref, ov_ref):
    nrow = xk_ref.shape[1]
    row = lax.broadcasted_iota(jnp.int32, (nrow, KV_WIDTH), 0)

    def compress(x_ref, pe_ref, w1_ref, w2_ref):
        x = x_ref[0].astype(F32)
        first = _dot((x + pe_ref[0:1, :]).astype(BF16), w1_ref[0])
        second = _dot((x + pe_ref[1:2, :]).astype(BF16), w1_ref[1])
        pre = first + pltpu.roll(second, nrow - 1, 0)
        return _dot(jax.nn.silu(pre).astype(BF16), w2_ref[...])

    valid = row < nrow - 1
    kc = _group_rms(compress(xk_ref, pek_ref, w1k_ref, w2k_ref), bd_ref[...]) * kgain_ref[...]
    ok_ref[0] = jnp.where(valid, kc, 0.0).astype(BF16)
    vc = compress(xv_ref, pev_ref, w1v_ref, w2v_ref)
    ov_ref[0] = jnp.where(valid, vc, 0.0).astype(BF16)


def _cmp_prep(xk16, xv16, pek, pev, w1k, w1v, w2k, w2v, kgain, bd):
    b, nrow, wide = xk16.shape
    full = lambda a: pl.BlockSpec(a.shape, lambda i: (0,) * a.ndim)
    blk = pl.BlockSpec((1, nrow, wide), lambda i: (i, 0, 0))
    oblk = pl.BlockSpec((1, nrow, KV_WIDTH), lambda i: (i, 0, 0))
    return pl.pallas_call(
        _cmp_kernel,
        grid=(b,),
        in_specs=[blk, blk] + [full(a) for a in (pek, pev, w1k, w1v, w2k, w2v, kgain, bd)],
        out_specs=[oblk, oblk],
        out_shape=[jax.ShapeDtypeStruct((b, nrow, KV_WIDTH), BF16)] * 2,
        compiler_params=pltpu.CompilerParams(dimension_semantics=("arbitrary",), vmem_limit_bytes=VMEM_LIMIT),
        name="cmp_prep",
    )(xk16, xv16, pek, pev, w1k, w1v, w2k, w2v, kgain, bd)


def _nsa_kernel(q_ref, kc_ref, vc_ref, kv_ref, gl_ref, ovt_ref,
                out_ref, vtc_ref, vts_ref, vtw_ref, sel_ref, sc_ref, m_ref, acc_ref):
    i = pl.program_id(1)
    n_slc = ovt_ref.shape[0]
    tq = Q_TILE
    kc = SLC_CHUNK
    nb = kc // SLC_LEN
    s_len = kv_ref.shape[1]
    cols = NSA_HPG * tq
    groups = range(NSA_GROUPS)

    def transposed_v(v, g):
        vt = v.astype(F32).T
        other = lax.broadcasted_iota(jnp.int32, vt.shape, 0) // HEAD_DIM != g
        return jnp.where(other, 1.0, vt).astype(BF16)

    @pl.when(i == 0)
    def _():
        for g in groups:
            vtc_ref[g] = transposed_v(vc_ref[0], g)
            for c in range(s_len // kc):
                vts_ref[g, c] = transposed_v(kv_ref[0, c * kc:(c + 1) * kc, KV_WIDTH:2 * KV_WIDTH], g)
            for c in range(s_len // tq):
                vtw_ref[g, c] = transposed_v(kv_ref[0, c * tq:(c + 1) * tq, 3 * KV_WIDTH:4 * KV_WIDTH], g)

    def tile4(a):
        return jnp.concatenate([a] * NSA_HPG, axis=1)

    def colmax(a):
        return jnp.max(a, axis=0, keepdims=True)

    q = q_ref[0]
    t_q = i * tq + lax.broadcasted_iota(jnp.int32, (1, tq), 1)
    gates_t = jax.nn.sigmoid(gl_ref[0].astype(F32)).T
    q4s = [jnp.concatenate([q[:, (NSA_HPG * g + h) * LANES:(NSA_HPG * g + h + 1) * LANES] for h in range(NSA_HPG)],
                           axis=0) for g in groups]

    nwt = pl.cdiv(WIN_LEN, tq) + 1
    wtile = jnp.maximum(i - (nwt - 1), 0)
    kpos = wtile * tq + lax.broadcasted_iota(jnp.int32, (nwt * tq, tq), 0)
    wbias = tile4(jnp.where(kpos <= t_q, jnp.where(kpos > t_q - WIN_LEN, 0.0, NEG_INF), NEG_INF))
    wrows = pl.ds(pl.multiple_of(wtile * tq, tq), nwt * tq)
    cmp_end = lax.broadcasted_iota(jnp.int32, (kc_ref.shape[1], tq), 0) * CMP_STRIDE + (CMP_LEN - 1)
    cmp_bias = tile4(jnp.where(cmp_end <= t_q, 0.0, NEG_INF))
    cmp_any = tile4(jnp.where(t_q >= CMP_LEN - 1, 1.0, 0.0))

    o_cmps, o_wins = [], []
    for g in groups:
        rs = slice(g * HEAD_DIM, (g + 1) * HEAD_DIM)
        ls = (1 - g) * HEAD_DIM

        sw = _dot_nt(kv_ref[0, wrows, 2 * KV_WIDTH:3 * KV_WIDTH], q4s[g]) + wbias
        vtw = jnp.concatenate([vtw_ref[g, wtile + j] for j in range(nwt)], axis=1)
        ow = _dot(vtw, jnp.exp((sw - colmax(sw)).astype(BF16)))
        o_wins.append(ow[rs] * (1.0 / ow[ls:ls + 1]))

        sc = _dot_nt(kc_ref[0], q4s[g]) + cmp_bias
        e = jnp.exp(sc - colmax(sc))
        scale = cmp_any * (1.0 / jnp.maximum(jnp.sum(e, axis=0, keepdims=True), 1e-30))
        p = e * scale
        o_cmps.append(_dot(vtc_ref[g], p.astype(BF16))[rs])
        psum = p[:, 0:tq]
        for h in range(1, NSA_HPG):
            psum = psum + p[:, h * tq:(h + 1) * tq]
        p_hi, p_lo = _split_bf16(psum)
        imp = _dot(ovt_ref[...], p_hi) + _dot(ovt_ref[...], p_lo)

        j_idx = lax.broadcasted_iota(jnp.int32, (n_slc, tq), 0)
        t_blk = (i * tq + lax.broadcasted_iota(jnp.int32, (n_slc, tq), 1)) // SLC_LEN
        forced = (j_idx == 0) | (j_idx == t_blk) | (j_idx == t_blk - 1)
        score = jnp.where(j_idx <= t_blk, imp + jnp.where(forced, FORCE_BONUS, 0.0), -1.0)
        blocks = [score[8 * v:8 * v + 8] for v in range(n_slc // 8)]
        jrow = lax.broadcasted_iota(jnp.int32, (8, tq), 0)
        ranks = [jnp.zeros((8, tq), F32) for _ in blocks]
        for k in range(n_slc):
            sk = score[k:k + 1, :]
            for v, sb in enumerate(blocks):
                if 8 * v > k:
                    ahead = sk >= sb
                elif 8 * v + 7 < k:
                    ahead = sk > sb
                else:
                    ahead = jnp.where(jrow + 8 * v > k, jnp.where(sk >= sb, 1.0, 0.0), jnp.where(sk > sb, 1.0, 0.0)) > 0.5
                ranks[v] = ranks[v] + jnp.where(ahead, 1.0, 0.0)
        for v, r in enumerate(ranks):
            sel = jnp.where(r < float(SLC_TOPK), 0.0, NEG_INF)
            sel_ref[g, 8 * v:8 * v + 8] = jnp.broadcast_to(sel[:, None, :], (8, 8, tq))

    n_chunks = (i * tq + tq - 1) // kc + 1
    m_ref[...] = jnp.full(m_ref.shape, NEG_INF, F32)
    acc_ref[...] = jnp.zeros_like(acc_ref)
    krow = lax.broadcasted_iota(jnp.int32, (kc, tq), 0)

    def score_chunk(c):
        krows = kv_ref[0, pl.ds(pl.multiple_of(c * kc, kc), kc), 0:KV_WIDTH]
        causal = (c * kc + krow) <= t_q
        for g in groups:
            chosen = jnp.concatenate(
                [jnp.broadcast_to(sel_ref[g, c * nb + jb][None], (SLC_LEN // 8, 8, tq)).reshape(SLC_LEN, tq)
                 for jb in range(nb)], axis=0)
            sc = _dot_nt(krows, q4s[g]) + tile4(jnp.where(causal, chosen, NEG_INF))
            sc_ref[g, c] = sc
            m_ref[g] = jnp.maximum(m_ref[g], jnp.max(sc.reshape(kc // 8, 8, cols), axis=0))

    def accumulate_chunk(c, m_prev):
        m_out = []
        for g in groups:
            m_cur = colmax(m_ref[g])
            e = jnp.exp((sc_ref[g, c] - m_cur).astype(BF16))
            acc_ref[g] = acc_ref[g] * jnp.exp(m_prev[g] - m_cur) + _dot(vts_ref[g, c], e)
            m_out.append(m_cur)
        return tuple(m_out)

    def slc_trip(c, m_prev):
        m_cur = accumulate_chunk(c, m_prev)
        score_chunk(c + 1)
        return m_cur

    score_chunk(0)
    m_last = lax.fori_loop(0, n_chunks - 1, slc_trip, tuple(jnp.full((1, cols), NEG_INF, F32) for _ in groups))
    accumulate_chunk(n_chunks - 1, m_last)

    ys = []
    for g in groups:
        rs = slice(g * HEAD_DIM, (g + 1) * HEAD_DIM)
        ls = (1 - g) * HEAD_DIM
        acc = acc_ref[g]
        o_slc = acc[rs] * (1.0 / acc[ls:ls + 1])
        for h in range(NSA_HPG):
            r = (NSA_HPG * g + h) * 3
            blk = slice(h * tq, (h + 1) * tq)
            ys.append(gates_t[r:r + 1] * o_cmps[g][:, blk] + gates_t[r + 1:r + 2] * o_slc[:, blk]
                      + gates_t[r + 2:r + 3] * o_wins[g][:, blk])
    out_ref[0] = jnp.concatenate(ys, axis=0).T.astype(BF16)


def _nsa(qn, kcmp, vcmp, kv4, glog, ovt):
    b, s, _ = qn.shape
    nq = s // Q_TILE
    ncmp = kcmp.shape[1]
    n_slc = ovt.shape[0]
    cols = NSA_HPG * Q_TILE
    full = lambda a: pl.BlockSpec(a.shape, lambda bi, i: (0,) * a.ndim)
    return pl.pallas_call(
        _nsa_kernel,
        grid=(b, nq),
        in_specs=[pl.BlockSpec((1, Q_TILE, NSA_HEADS * LANES), lambda bi, i: (bi, i, 0)),
                  pl.BlockSpec((1, ncmp, KV_WIDTH), lambda bi, i: (bi, 0, 0)),
                  pl.BlockSpec((1, ncmp, KV_WIDTH), lambda bi, i: (bi, 0, 0)),
                  pl.BlockSpec((1, s, 4 * KV_WIDTH), lambda bi, i: (bi, 0, 0)),
                  pl.BlockSpec((1, Q_TILE, LANES), lambda bi, i: (bi, i, 0)),
                  full(ovt)],
        out_specs=pl.BlockSpec((1, Q_TILE, NSA_WIDTH), lambda bi, i: (bi, i, 0)),
        out_shape=jax.ShapeDtypeStruct((b, s, NSA_WIDTH), BF16),
        scratch_shapes=[pltpu.VMEM((NSA_GROUPS, KV_WIDTH, ncmp), BF16),
                        pltpu.VMEM((NSA_GROUPS, s // SLC_CHUNK, KV_WIDTH, SLC_CHUNK), BF16),
                        pltpu.VMEM((NSA_GROUPS, nq, KV_WIDTH, Q_TILE), BF16),
                        pltpu.VMEM((NSA_GROUPS, n_slc, 8, Q_TILE), F32),
                        pltpu.VMEM((NSA_GROUPS, s // SLC_CHUNK, SLC_CHUNK, cols), F32),
                        pltpu.VMEM((NSA_GROUPS, 8, cols), F32),
                        pltpu.VMEM((NSA_GROUPS, KV_WIDTH, cols), F32)],
        compiler_params=pltpu.CompilerParams(dimension_semantics=("arbitrary", "arbitrary"),
                                             vmem_limit_bytes=VMEM_LIMIT),
        name="nsa",
    )(qn, kcmp, vcmp, kv4, glog, ovt)


def _outffn_kernel(x_ref, yrc_ref, yn_ref, woa_ref, wob_ref, gain_ref, wup_ref, wdn_ref, out_ref):
    x1 = x_ref[...] + _dot(yrc_ref[...], woa_ref[...]) + _dot(yn_ref[...], wob_ref[...])
    h = x1 * lax.rsqrt(jnp.mean(x1 * x1, axis=-1, keepdims=True) + EPS) * gain_ref[...]
    h = h.astype(BF16)
    out_ref[...] = x1
    for c in range(wup_ref.shape[1] // FF_CHUNK):
        sl = slice(c * FF_CHUNK, (c + 1) * FF_CHUNK)
        u = jnp.maximum(_dot(h, wup_ref[:, sl]), 0.0)
        out_ref[...] += _dot((u * u).astype(BF16), wdn_ref[sl, :])


def _outffn(x2d, yrc, yn, woa, wob, gain, wup, wdn):
    t, d = x2d.shape
    tm = ROW_TILE
    row = lambda w: pl.BlockSpec((tm, w), lambda i: (i, 0))
    full = lambda a: pl.BlockSpec(a.shape, lambda i: (0,) * a.ndim)
    return pl.pallas_call(
        _outffn_kernel,
        grid=(t // tm,),
        in_specs=[row(d), row(yrc.shape[1]), row(yn.shape[1])] + [full(a) for a in (woa, wob, gain, wup, wdn)],
        out_specs=row(d),
        out_shape=jax.ShapeDtypeStruct((t, d), F32),
        compiler_params=pltpu.CompilerParams(dimension_semantics=("arbitrary",), vmem_limit_bytes=VMEM_LIMIT),
        name="outffn",
    )(x2d, yrc, yn, woa, wob, gain, wup, wdn)


def _block_diag_ones(n, block, value):
    idx = np.arange(n) // block
    return jnp.asarray((idx[:, None] == idx[None, :]).astype(np.float32) * value)


def _retention_tables(s):
    h, d, c = RET_HEADS, HEAD_DIM, RET_CHUNK
    pos = jnp.arange(s, dtype=F32)
    inv_freq = 1.0 / (XPOS_BASE ** jnp.linspace(0.0, 1.0, d // 2, dtype=F32))
    ang = jnp.repeat(pos[:, None] * inv_freq[None, :], 2, axis=-1)
    sign = jnp.tile(jnp.asarray([-1.0, 1.0], F32), d // 2)
    cos = jnp.tile(jnp.cos(ang), (1, h))
    sin = jnp.tile(jnp.sin(ang) * sign[None, :], (1, h))
    log_gamma = jnp.log(1.0 - 2.0 ** (-5.0 - jnp.arange(h, dtype=F32)))
    idx = jnp.arange(c, dtype=F32)
    rel = idx[:, None] - idx[None, :]
    dmask = jnp.where(rel >= 0, jnp.exp(log_gamma[:, None, None] * jnp.maximum(rel, 0.0)), 0.0)
    kdec = jnp.repeat(jnp.exp(log_gamma[:, None] * (c - 1.0 - idx)).T, d, axis=1)
    qdec = jnp.repeat(jnp.exp(log_gamma[:, None] * (idx + 1.0)).T, d, axis=1)
    bdm = _block_diag_ones(h * d, d, 1.0)
    sdec = bdm * jnp.repeat(jnp.exp(log_gamma * c), d)[:, None]
    return cos, sin, dmask, qdec, kdec, sdec, bdm


def _nsa_tables(s):
    n_rows = s // CMP_STRIDE
    n_cmp = (s - CMP_LEN) // CMP_STRIDE + 1
    n_slc = s // SLC_LEN
    cs = np.arange(n_cmp)[:, None] * CMP_STRIDE
    ss = np.arange(n_slc)[None, :] * SLC_LEN
    overlap = np.clip(np.minimum(cs + CMP_LEN, ss + SLC_LEN) - np.maximum(cs, ss), 0, None).astype(np.float32) / CMP_LEN
    ovt = np.zeros((n_slc, n_rows), np.float32)
    ovt[:, :n_cmp] = overlap.T
    return jnp.asarray(ovt, BF16)


def _cmp_weights(pe, w1, w2):
    half = CMP_LEN // 2
    eye = jnp.eye(NSA_GROUPS, dtype=F32)
    w1r = w1.reshape(2, half, HEAD_DIM, HEAD_DIM)
    w1big = jnp.einsum('arce,gh->argche', w1r, eye).reshape(2, half * KV_WIDTH, KV_WIDTH)
    pe2 = jnp.broadcast_to(pe.reshape(2, half, 1, HEAD_DIM), (2, half, NSA_GROUPS, HEAD_DIM)).reshape(2, half * KV_WIDTH)
    w2big = jnp.kron(eye, w2)
    return pe2, w1big.astype(BF16), w2big.astype(BF16)


def kernel(x, norm_mix, w_in, conv_w, nsa_q_norm, nsa_k_norm, cmp_pe_k, cmp_w1_k, cmp_w2_k,
           cmp_pe_v, cmp_w1_v, cmp_w2_v, w_out, norm_mlp, w_up, w_down):
    b, s, d = x.shape
    t = b * s
    depth = w_in.shape[0]
    ret_tabs = _retention_tables(s)
    ovt = _nsa_tables(s)
    bd128 = _block_diag_ones(KV_WIDTH, HEAD_DIM, 1.0 / HEAD_DIM).astype(BF16)
    bd256 = _block_diag_ones(RET_WIDTH, HEAD_DIM, 1.0 / HEAD_DIM).astype(BF16)
    c_rc = 4 * RET_WIDTH + 3 * CONV_WIDTH
    c_q = c_rc + NSA_WIDTH
    c_kc = c_q + KV_WIDTH
    c_vc = c_kc + KV_WIDTH
    c_kv = c_vc + 4 * KV_WIDTH
    n_gate = 3 * NSA_HEADS
    rows16 = s // CMP_STRIDE
    n_rc = RET_WIDTH + CONV_WIDTH

    xf = x.reshape(t, d)
    for l in range(depth):
        w = w_in[l].astype(BF16)
        wg = jnp.pad(w[:, c_kv:c_kv + n_gate], ((0, 0), (0, LANES - n_gate)))
        qgain = jnp.tile(nsa_q_norm[l], RET_WIDTH // HEAD_DIM)[None, :]
        kgain = jnp.tile(nsa_k_norm[l], NSA_GROUPS)[None, :]
        z_rc, qn, zkc, zvc, kv4, glog = _inproj(
            xf, norm_mix[l][None, :], w[:, :c_rc], w[:, c_rc:c_q], w[:, c_q:c_kc], w[:, c_kc:c_vc], w[:, c_vc:c_kv], wg,
            qgain, jnp.tile(kgain, (1, 2)), bd256)
        y_rc = _retconv(z_rc.reshape(b, s, c_rc), *ret_tabs, bd256, conv_w[l])
        pek, w1k, w2k = _cmp_weights(cmp_pe_k[l], cmp_w1_k[l], cmp_w2_k[l])
        pev, w1v, w2v = _cmp_weights(cmp_pe_v[l], cmp_w1_v[l], cmp_w2_v[l])
        kcmp, vcmp = _cmp_prep(zkc.reshape(b, rows16, CMP_STRIDE * KV_WIDTH), zvc.reshape(b, rows16, CMP_STRIDE * KV_WIDTH),
                               pek, pev, w1k, w1v, w2k, w2v, kgain, bd128)
        y_nsa = _nsa(qn.reshape(b, s, NSA_HEADS * LANES), kcmp, vcmp, kv4.reshape(b, s, 4 * KV_WIDTH),
                     glog.reshape(b, s, LANES), ovt)
        wo = w_out[l].astype(BF16)
        xf = _outffn(xf, y_rc.reshape(t, n_rc), y_nsa.reshape(t, NSA_WIDTH), wo[:n_rc], wo[n_rc:],
                     norm_mlp[l][None, :], w_up[l].astype(BF16), w_down[l].astype(BF16))
    return xf.reshape(b, s, d)
```

```python
import functools

import numpy as np
import jax
import jax.numpy as jnp
from jax import lax
from jax.experimental import pallas as pl
from jax.experimental.pallas import tpu as pltpu

F32 = jnp.float32
BF16 = jnp.bfloat16

HEAD_DIM = 64
LANES = 128
RET_CHUNK = 128
RET_UNROLL = 8
RET_HEADS = 4
RET_WIDTH = RET_HEADS * HEAD_DIM
CONV_WIDTH = 256
NSA_HEADS = 8
NSA_GROUPS = 2
NSA_HPG = NSA_HEADS // NSA_GROUPS
NSA_WIDTH = NSA_HEADS * HEAD_DIM
KV_WIDTH = NSA_GROUPS * HEAD_DIM
CMP_LEN = 32
CMP_STRIDE = 16
SLC_LEN = 64
SLC_TOPK = 8
WIN_LEN = 256
Q_TILE = 256
SLC_CHUNK = 512
ROW_TILE = 512
FF_CHUNK = 1024
XPOS_BASE = 10000.0
EPS = 1e-6
NEG_INF = -1e30
FORCE_BONUS = 1e4
VMEM_LIMIT = 56 * 1024 * 1024


def _dot(a, b):
    return jnp.dot(a, b, preferred_element_type=F32)


def _dot_nt(a, b):
    return lax.dot_general(a, b, (((1,), (1,)), ((), ())), preferred_element_type=F32)


def _dot_tn(a, b):
    return lax.dot_general(a, b, (((0,), (0,)), ((), ())), preferred_element_type=F32)


def _split_bf16(x):
    hi = x.astype(BF16)
    lo = (x - hi.astype(F32)).astype(BF16)
    return hi, lo


def _group_rms(z, bd):
    return z * lax.rsqrt(_dot((z * z).astype(BF16), bd) + EPS)


def _inproj_kernel(x_ref, gain_ref, w_ref, qgain_ref, kgain_ref, bd_ref,
                   orc_ref, oq_ref, okc_ref, ovc_ref, okv_ref, og_ref, stk_ref, stv_ref):
    x = x_ref[...]
    h = x * lax.rsqrt(jnp.mean(x * x, axis=-1, keepdims=True) + EPS) * gain_ref[...]
    h = h.astype(BF16)
    bd = bd_ref[...]
    wide = bd.shape[0]
    z = _dot(h, w_ref[...])
    kw = KV_WIDTH
    c_kc = NSA_WIDTH
    c_kv = c_kc + 2 * kw
    c_g = c_kv + 4 * kw
    c_rc = c_g + og_ref.shape[1]
    zq = z[:, 0:NSA_WIDTH]
    qscale = qgain_ref[...] * (HEAD_DIM ** -0.5)
    lane_half = lax.broadcasted_iota(jnp.int32, (x.shape[0], LANES), 1) // HEAD_DIM
    for g in range(NSA_WIDTH // wide):
        qn = _group_rms(zq[:, g * wide:(g + 1) * wide], bd) * qscale
        for hh in range(NSA_HPG):
            blk = qn[:, (hh // 2) * LANES:(hh // 2 + 1) * LANES]
            if hh % 2 != g:
                blk = pltpu.roll(blk, HEAD_DIM, 1)
            hd = NSA_HPG * g + hh
            oq_ref[:, hd * LANES:(hd + 1) * LANES] = jnp.where(lane_half == g, blk, 0.0).astype(BF16)
    for c0, o_ref, st_ref in ((c_kc, okc_ref, stk_ref), (c_kc + kw, ovc_ref, stv_ref)):
        st_ref[...] = z[:, c0:c0 + kw]
        for r in range(CMP_STRIDE):
            o_ref[:, r * kw:(r + 1) * kw] = st_ref[pl.ds(r, x.shape[0] // CMP_STRIDE, stride=CMP_STRIDE), :].astype(BF16)
    zkv = z[:, c_kv:c_g]
    kn = _group_rms(jnp.concatenate([zkv[:, 0:kw], zkv[:, 2 * kw:3 * kw]], axis=1), bd) * kgain_ref[...]
    okv_ref[:, 0:kw] = kn[:, 0:kw].astype(BF16)
    okv_ref[:, kw:2 * kw] = zkv[:, kw:2 * kw].astype(BF16)
    okv_ref[:, 2 * kw:3 * kw] = kn[:, kw:2 * kw].astype(BF16)
    okv_ref[:, 3 * kw:4 * kw] = zkv[:, 3 * kw:4 * kw].astype(BF16)
    og_ref[...] = z[:, c_g:c_rc].astype(BF16)
    orc_ref[...] = z[:, c_rc:].astype(BF16)


def _inproj(x2d, gain, w_all, qgain, kgain, bd, n_rc, n_kv, n_gate):
    t, d = x2d.shape
    tm = ROW_TILE
    row = lambda w: pl.BlockSpec((tm, w), lambda i: (i, 0))
    full = lambda a: pl.BlockSpec(a.shape, lambda i: (0,) * a.ndim)
    wide16 = CMP_STRIDE * KV_WIDTH
    rows16 = pl.BlockSpec((tm // CMP_STRIDE, wide16), lambda i: (i, 0))
    sds16 = jax.ShapeDtypeStruct((t // CMP_STRIDE, wide16), BF16)
    widths = (n_rc, NSA_HEADS * LANES, None, None, n_kv, n_gate)
    return pl.pallas_call(
        _inproj_kernel,
        grid=(t // tm,),
        in_specs=[row(d)] + [full(a) for a in (gain, w_all, qgain, kgain, bd)],
        out_specs=[rows16 if w is None else row(w) for w in widths],
        out_shape=[sds16 if w is None else jax.ShapeDtypeStruct((t, w), BF16) for w in widths],
        scratch_shapes=[pltpu.VMEM((tm, KV_WIDTH), F32)] * 2,
        compiler_params=pltpu.CompilerParams(dimension_semantics=("arbitrary",), vmem_limit_bytes=VMEM_LIMIT),
        name="inproj",
    )(x2d, gain, w_all, qgain, kgain, bd)


def _retconv_kernel(z_ref, cos_ref, sin_ref, dmask_ref, qdec_ref, kdec_ref, sdec_ref, bdm_ref, bdn_ref, convw_ref,
                    out_ref, state_ref):
    s_len = z_ref.shape[1]
    w = RET_WIDTH
    lane = lax.broadcasted_iota(jnp.int32, (RET_CHUNK, w), 1)
    even = lane % 2 == 0
    head = lane // HEAD_DIM

    def rotate(xb, cos, sin):
        x = xb.astype(F32)
        swapped = jnp.where(even, pltpu.roll(x, w - 1, 1), pltpu.roll(x, 1, 1))
        return x * cos + swapped * sin

    state_ref[...] = jnp.zeros_like(state_ref)

    def chunk(c, carry):
        r0 = pl.multiple_of(c * RET_CHUNK, RET_CHUNK)
        rows = pl.ds(r0, RET_CHUNK)
        cos = cos_ref[rows, :]
        sin = sin_ref[rows, :]
        q = rotate(z_ref[0, rows, 0:w], cos, sin)
        k = rotate(z_ref[0, rows, w:2 * w], cos, sin) * (HEAD_DIM ** -0.5)
        vb = z_ref[0, rows, 2 * w:3 * w]
        gate = z_ref[0, rows, 3 * w:4 * w].astype(F32)
        qb = q.astype(BF16)
        kb = k.astype(BF16)
        zero = jnp.zeros_like(qb)
        o = jnp.zeros((RET_CHUNK, w), F32)
        for h in range(RET_HEADS):
            hm = head == h
            scores = _dot_nt(jnp.where(hm, qb, zero), kb) * dmask_ref[h]
            o = o + _dot(scores.astype(BF16), jnp.where(hm, vb, zero))
        state = state_ref[...]
        o = o + _dot((q * qdec_ref[...]).astype(BF16), state.astype(BF16))
        kv = _dot_tn((k * kdec_ref[...]).astype(BF16), vb)
        state_ref[...] = state * sdec_ref[...] + kv * bdm_ref[...]
        o = _group_rms(o, bdn_ref[...])
        y = gate * jax.nn.sigmoid(gate) * o
        out_ref[0, rows, 0:w] = y.astype(BF16)
        return carry

    lax.fori_loop(0, s_len // RET_CHUNK, chunk, 0, unroll=RET_UNROLL)

    c0 = 4 * w
    cw = CONV_WIDTH
    b_gate = z_ref[0, :, c0:c0 + cw].astype(F32)
    u = z_ref[0, :, c0 + cw:c0 + 2 * cw].astype(F32) * z_ref[0, :, c0 + 2 * cw:c0 + 3 * cw].astype(F32)
    t_idx = lax.broadcasted_iota(jnp.int32, (s_len, cw), 0)
    u1 = jnp.where(t_idx >= 1, pltpu.roll(u, 1, 0), 0.0)
    u2 = jnp.where(t_idx >= 2, pltpu.roll(u, 2, 0), 0.0)
    y = b_gate * (convw_ref[0:1, :] * u2 + convw_ref[1:2, :] * u1 + convw_ref[2:3, :] * u)
    out_ref[0, :, w:w + cw] = y.astype(BF16)


def _retconv(z_rc, cos, sin, dmask, qdec, kdec, sdec, bdm, bdn, convw):
    b, s, wz = z_rc.shape
    full = lambda a: pl.BlockSpec(a.shape, lambda i: (0,) * a.ndim)
    return pl.pallas_call(
        _retconv_kernel,
        grid=(b,),
        in_specs=[pl.BlockSpec((1, s, wz), lambda i: (i, 0, 0))]
        + [full(a) for a in (cos, sin, dmask, qdec, kdec, sdec, bdm, bdn, convw)],
        out_specs=pl.BlockSpec((1, s, RET_WIDTH + CONV_WIDTH), lambda i: (i, 0, 0)),
        out_shape=jax.ShapeDtypeStruct((b, s, RET_WIDTH + CONV_WIDTH), BF16),
        scratch_shapes=[pltpu.VMEM((RET_WIDTH, RET_WIDTH), F32)],
        compiler_params=pltpu.CompilerParams(dimension_semantics=("arbitrary",), vmem_limit_bytes=VMEM_LIMIT),
        name="retconv",
    )(z_rc, cos, sin, dmask, qdec, kdec, sdec, bdm, bdn, convw)


def _cmp_kernel(xk_ref, xv_ref, pek_ref, pev_ref, w1k_ref, w1v_ref, w2k_ref, w2v_ref, kgain_ref, bd_ref,
                ok_ref, ov_ref):
    nrow = xk_ref.shape[1]
    row = lax.broadcasted_iota(jnp.int32, (nrow, KV_WIDTH), 0)

    def compress(x_ref, pe_ref, w1_ref, w2_ref):
        x = x_ref[0].astype(F32)
        first = _dot((x + pe_ref[0:1, :]).astype(BF16), w1_ref[0])
        second = _dot((x + pe_ref[1:2, :]).astype(BF16), w1_ref[1])
        pre = first + pltpu.roll(second, nrow - 1, 0)
        return _dot(jax.nn.silu(pre).astype(BF16), w2_ref[...])

    valid = row < nrow - 1
    kc = _group_rms(compress(xk_ref, pek_ref, w1k_ref, w2k_ref), bd_ref[...]) * kgain_ref[...]
    ok_ref[0] = jnp.where(valid, kc, 0.0).astype(BF16)
    vc = compress(xv_ref, pev_ref, w1v_ref, w2v_ref)
    ov_ref[0] = jnp.where(valid, vc, 0.0).astype(BF16)


def _cmp_prep(xk16, xv16, pek, pev, w1k, w1v, w2k, w2v, kgain, bd):
    b, nrow, wide = xk16.shape
    full = lambda a: pl.BlockSpec(a.shape, lambda i: (0,) * a.ndim)
    blk = pl.BlockSpec((1, nrow, wide), lambda i: (i, 0, 0))
    oblk = pl.BlockSpec((1, nrow, KV_WIDTH), lambda i: (i, 0, 0))
    return pl.pallas_call(
        _cmp_kernel,
        grid=(b,),
        in_specs=[blk, blk] + [full(a) for a in (pek, pev, w1k, w1v, w2k, w2v, kgain, bd)],
        out_specs=[oblk, oblk],
        out_shape=[jax.ShapeDtypeStruct((b, nrow, KV_WIDTH), BF16)] * 2,
        compiler_params=pltpu.CompilerParams(dimension_semantics=("arbitrary",), vmem_limit_bytes=VMEM_LIMIT),
        name="cmp_prep",
    )(xk16, xv16, pek, pev, w1k, w1v, w2k, w2v, kgain, bd)


def _nsa_kernel(q_ref, kc_ref, vc_ref, kv_ref, gl_ref, ovt_ref,
                out_ref, vtc_ref, vts_ref, vtw_ref, sel_ref, sc_ref, m_ref, acc_ref):
    i = pl.program_id(1)
    n_slc = ovt_ref.shape[0]
    tq = Q_TILE
    kc = SLC_CHUNK
    nb = kc // SLC_LEN
    s_len = kv_ref.shape[1]
    cols = NSA_HPG * tq
    groups = range(NSA_GROUPS)

    def transposed_v(v, g):
        vt = v.astype(F32).T
        other = lax.broadcasted_iota(jnp.int32, vt.shape, 0) // HEAD_DIM != g
        return jnp.where(other, 1.0, vt).astype(BF16)

    @pl.when(i == 0)
    def _():
        for g in groups:
            vtc_ref[g] = transposed_v(vc_ref[0], g)
            for c in range(s_len // kc):
                vts_ref[g, c] = transposed_v(kv_ref[0, c * kc:(c + 1) * kc, KV_WIDTH:2 * KV_WIDTH], g)
            for c in range(s_len // tq):
                vtw_ref[g, c] = transposed_v(kv_ref[0, c * tq:(c + 1) * tq, 3 * KV_WIDTH:4 * KV_WIDTH], g)

    def tile4(a):
        return jnp.concatenate([a] * NSA_HPG, axis=1)

    def colmax(a):
        return jnp.max(a, axis=0, keepdims=True)

    q = q_ref[0]
    t_q = i * tq + lax.broadcasted_iota(jnp.int32, (1, tq), 1)
    gates_t = jax.nn.sigmoid(gl_ref[0].astype(F32)).T
    q4s = [jnp.concatenate([q[:, (NSA_HPG * g + h) * LANES:(NSA_HPG * g + h + 1) * LANES] for h in range(NSA_HPG)],
                           axis=0) for g in groups]

    nwt = pl.cdiv(WIN_LEN, tq) + 1
    wtile = jnp.maximum(i - (nwt - 1), 0)
    kpos = wtile * tq + lax.broadcasted_iota(jnp.int32, (nwt * tq, tq), 0)
    wbias = tile4(jnp.where(kpos <= t_q, jnp.where(kpos > t_q - WIN_LEN, 0.0, NEG_INF), NEG_INF))
    wrows = pl.ds(pl.multiple_of(wtile * tq, tq), nwt * tq)
    cmp_end = lax.broadcasted_iota(jnp.int32, (kc_ref.shape[1], tq), 0) * CMP_STRIDE + (CMP_LEN - 1)
    cmp_bias = tile4(jnp.where(cmp_end <= t_q, 0.0, NEG_INF))
    cmp_any = tile4(jnp.where(t_q >= CMP_LEN - 1, 1.0, 0.0))

    o_cmps, o_wins = [], []
    for g in groups:
        rs = slice(g * HEAD_DIM, (g + 1) * HEAD_DIM)
        ls = (1 - g) * HEAD_DIM

        sw = _dot_nt(kv_ref[0, wrows, 2 * KV_WIDTH:3 * KV_WIDTH], q4s[g]) + wbias
        vtw = jnp.concatenate([vtw_ref[g, wtile + j] for j in range(nwt)], axis=1)
        ow = _dot(vtw, jnp.exp((sw - colmax(sw)).astype(BF16)))
        o_wins.append(ow[rs] * (1.0 / ow[ls:ls + 1]))

        sc = _dot_nt(kc_ref[0], q4s[g]) + cmp_bias
        e = jnp.exp(sc - colmax(sc))
        scale = cmp_any * (1.0 / jnp.maximum(jnp.sum(e, axis=0, keepdims=True), 1e-30))
        p = e * scale
        o_cmps.append(_dot(vtc_ref[g], p.astype(BF16))[rs])
        psum = p[:, 0:tq]
        for h in range(1, NSA_HPG):
            psum = psum + p[:, h * tq:(h + 1) * tq]
        p_hi, p_lo = _split_bf16(psum)
        imp = _dot(ovt_ref[...], p_hi) + _dot(ovt_ref[...], p_lo)

        j_idx = lax.broadcasted_iota(jnp.int32, (n_slc, tq), 0)
        t_blk = (i * tq + lax.broadcasted_iota(jnp.int32, (n_slc, tq), 1)) // SLC_LEN
        forced = (j_idx == 0) | (j_idx == t_blk) | (j_idx == t_blk - 1)
        score = jnp.where(j_idx <= t_blk, imp + jnp.where(forced, FORCE_BONUS, 0.0), -1.0)
        blocks = [score[8 * v:8 * v + 8] for v in range(n_slc // 8)]
        jrow = lax.broadcasted_iota(jnp.int32, (8, tq), 0)
        ranks = [jnp.zeros((8, tq), F32) for _ in blocks]
        for k in range(n_slc):
            sk = score[k:k + 1, :]
            for v, sb in enumerate(blocks):
                if 8 * v > k:
                    ahead = sk >= sb
                elif 8 * v + 7 < k:
                    ahead = sk > sb
                else:
                    ahead = jnp.where(jrow + 8 * v > k, jnp.where(sk >= sb, 1.0, 0.0), jnp.where(sk > sb, 1.0, 0.0)) > 0.5
                ranks[v] = ranks[v] + jnp.where(ahead, 1.0, 0.0)
        for v, r in enumerate(ranks):
            sel = jnp.where(r < float(SLC_TOPK), 0.0, NEG_INF)
            sel_ref[g, 8 * v:8 * v + 8] = jnp.broadcast_to(sel[:, None, :], (8, 8, tq))

    n_chunks = (i * tq + tq - 1) // kc + 1
    m_ref[...] = jnp.full(m_ref.shape, NEG_INF, F32)
    acc_ref[...] = jnp.zeros_like(acc_ref)
    krow = lax.broadcasted_iota(jnp.int32, (kc, tq), 0)

    def score_chunk(c):
        krows = kv_ref[0, pl.ds(pl.multiple_of(c * kc, kc), kc), 0:KV_WIDTH]
        causal = (c * kc + krow) <= t_q
        for g in groups:
            chosen = jnp.concatenate(
                [jnp.broadcast_to(sel_ref[g, c * nb + jb][None], (SLC_LEN // 8, 8, tq)).reshape(SLC_LEN, tq)
                 for jb in range(nb)], axis=0)
            sc = _dot_nt(krows, q4s[g]) + tile4(jnp.where(causal, chosen, NEG_INF))
            sc_ref[g, c] = sc
            m_ref[g] = jnp.maximum(m_ref[g], jnp.max(sc.reshape(kc // 8, 8, cols), axis=0))

    def accumulate_chunk(c, m_prev):
        m_out = []
        for g in groups:
            m_cur = colmax(m_ref[g])
            e = jnp.exp((sc_ref[g, c] - m_cur).astype(BF16))
            acc_ref[g] = acc_ref[g] * jnp.exp(m_prev[g] - m_cur) + _dot(vts_ref[g, c], e)
            m_out.append(m_cur)
        return tuple(m_out)

    def slc_trip(c, m_prev):
        m_cur = accumulate_chunk(c, m_prev)
        score_chunk(c + 1)
        return m_cur

    score_chunk(0)
    m_last = lax.fori_loop(0, n_chunks - 1, slc_trip, tuple(jnp.full((1, cols), NEG_INF, F32) for _ in groups))
    accumulate_chunk(n_chunks - 1, m_last)

    ys = []
    for g in groups:
        rs = slice(g * HEAD_DIM, (g + 1) * HEAD_DIM)
        ls = (1 - g) * HEAD_DIM
        acc = acc_ref[g]
        o_slc = acc[rs] * (1.0 / acc[ls:ls + 1])
        for h in range(NSA_HPG):
            r = (NSA_HPG * g + h) * 3
            blk = slice(h * tq, (h + 1) * tq)
            ys.append(gates_t[r:r + 1] * o_cmps[g][:, blk] + gates_t[r + 1:r + 2] * o_slc[:, blk]
                      + gates_t[r + 2:r + 3] * o_wins[g][:, blk])
    out_ref[0] = jnp.concatenate(ys, axis=0).T.astype(BF16)


def _nsa(qn, kcmp, vcmp, kv4, glog, ovt):
    b, s, _ = qn.shape
    nq = s // Q_TILE
    ncmp = kcmp.shape[1]
    n_slc = ovt.shape[0]
    cols = NSA_HPG * Q_TILE
    full = lambda a: pl.BlockSpec(a.shape, lambda bi, i: (0,) * a.ndim)
    return pl.pallas_call(
        _nsa_kernel,
        grid=(b, nq),
        in_specs=[pl.BlockSpec((1, Q_TILE, NSA_HEADS * LANES), lambda bi, i: (bi, i, 0)),
                  pl.BlockSpec((1, ncmp, KV_WIDTH), lambda bi, i: (bi, 0, 0)),
                  pl.BlockSpec((1, ncmp, KV_WIDTH), lambda bi, i: (bi, 0, 0)),
                  pl.BlockSpec((1, s, 4 * KV_WIDTH), lambda bi, i: (bi, 0, 0)),
                  pl.BlockSpec((1, Q_TILE, LANES), lambda bi, i: (bi, i, 0)),
                  full(ovt)],
        out_specs=pl.BlockSpec((1, Q_TILE, NSA_WIDTH), lambda bi, i: (bi, i, 0)),
        out_shape=jax.ShapeDtypeStruct((b, s, NSA_WIDTH), BF16),
        scratch_shapes=[pltpu.VMEM((NSA_GROUPS, KV_WIDTH, ncmp), BF16),
                        pltpu.VMEM((NSA_GROUPS, s // SLC_CHUNK, KV_WIDTH, SLC_CHUNK), BF16),
                        pltpu.VMEM((NSA_GROUPS, nq, KV_WIDTH, Q_TILE), BF16),
                        pltpu.VMEM((NSA_GROUPS, n_slc, 8, Q_TILE), F32),
                        pltpu.VMEM((NSA_GROUPS, s // SLC_CHUNK, SLC_CHUNK, cols), F32),
                        pltpu.VMEM((NSA_GROUPS, 8, cols), F32),
                        pltpu.VMEM((NSA_GROUPS, KV_WIDTH, cols), F32)],
        compiler_params=pltpu.CompilerParams(dimension_semantics=("arbitrary", "arbitrary"),
                                             vmem_limit_bytes=VMEM_LIMIT),
        name="nsa",
    )(qn, kcmp, vcmp, kv4, glog, ovt)


def _outffn_kernel(x_ref, yrc_ref, yn_ref, woa_ref, wob_ref, gain_ref, wup_ref, wdn_ref, out_ref):
    x1 = x_ref[...] + _dot(yrc_ref[...], woa_ref[...]) + _dot(yn_ref[...], wob_ref[...])
    h = x1 * lax.rsqrt(jnp.mean(x1 * x1, axis=-1, keepdims=True) + EPS) * gain_ref[...]
    h = h.astype(BF16)
    out_ref[...] = x1
    for c in range(wup_ref.shape[1] // FF_CHUNK):
        sl = slice(c * FF_CHUNK, (c + 1) * FF_CHUNK)
        u = jnp.maximum(_dot(h, wup_ref[:, sl]), 0.0)
        out_ref[...] += _dot((u * u).astype(BF16), wdn_ref[sl, :])


def _outffn(x2d, yrc, yn, woa, wob, gain, wup, wdn):
    t, d = x2d.shape
    tm = ROW_TILE
    row = lambda w: pl.BlockSpec((tm, w), lambda i: (i, 0))
    full = lambda a: pl.BlockSpec(a.shape, lambda i: (0,) * a.ndim)
    return pl.pallas_call(
        _outffn_kernel,
        grid=(t // tm,),
        in_specs=[row(d), row(yrc.shape[1]), row(yn.shape[1])] + [full(a) for a in (woa, wob, gain, wup, wdn)],
        out_specs=row(d),
        out_shape=jax.ShapeDtypeStruct((t, d), F32),
        compiler_params=pltpu.CompilerParams(dimension_semantics=("arbitrary",), vmem_limit_bytes=VMEM_LIMIT),
        name="outffn",
    )(x2d, yrc, yn, woa, wob, gain, wup, wdn)


def _block_diag_ones(n, block, value):
    idx = np.arange(n) // block
    return jnp.asarray((idx[:, None] == idx[None, :]).astype(np.float32) * value)


def _retention_tables(s):
    h, d, c = RET_HEADS, HEAD_DIM, RET_CHUNK
    pos = jnp.arange(s, dtype=F32)
    inv_freq = 1.0 / (XPOS_BASE ** jnp.linspace(0.0, 1.0, d // 2, dtype=F32))
    ang = jnp.repeat(pos[:, None] * inv_freq[None, :], 2, axis=-1)
    sign = jnp.tile(jnp.asarray([-1.0, 1.0], F32), d // 2)
    cos = jnp.tile(jnp.cos(ang), (1, h))
    sin = jnp.tile(jnp.sin(ang) * sign[None, :], (1, h))
    log_gamma = jnp.log(1.0 - 2.0 ** (-5.0 - jnp.arange(h, dtype=F32)))
    idx = jnp.arange(c, dtype=F32)
    rel = idx[:, None] - idx[None, :]
    dmask = jnp.where(rel >= 0, jnp.exp(log_gamma[:, None, None] * jnp.maximum(rel, 0.0)), 0.0)
    kdec = jnp.repeat(jnp.exp(log_gamma[:, None] * (c - 1.0 - idx)).T, d, axis=1)
    qdec = jnp.repeat(jnp.exp(log_gamma[:, None] * (idx + 1.0)).T, d, axis=1)
    bdm = _block_diag_ones(h * d, d, 1.0)
    sdec = bdm * jnp.repeat(jnp.exp(log_gamma * c), d)[:, None]
    return cos, sin, dmask, qdec, kdec, sdec, bdm


def _nsa_tables(s):
    n_rows = s // CMP_STRIDE
    n_cmp = (s - CMP_LEN) // CMP_STRIDE + 1
    n_slc = s // SLC_LEN
    cs = np.arange(n_cmp)[:, None] * CMP_STRIDE
    ss = np.arange(n_slc)[None, :] * SLC_LEN
    overlap = np.clip(np.minimum(cs + CMP_LEN, ss + SLC_LEN) - np.maximum(cs, ss), 0, None).astype(np.float32) / CMP_LEN
    ovt = np.zeros((n_slc, n_rows), np.float32)
    ovt[:, :n_cmp] = overlap.T
    return jnp.asarray(ovt, BF16)


def _cmp_weights(pe, w1, w2):
    half = CMP_LEN // 2
    eye = jnp.eye(NSA_GROUPS, dtype=F32)
    w1r = w1.reshape(2, half, HEAD_DIM, HEAD_DIM)
    w1big = jnp.einsum('arce,gh->argche', w1r, eye).reshape(2, half * KV_WIDTH, KV_WIDTH)
    pe2 = jnp.broadcast_to(pe.reshape(2, half, 1, HEAD_DIM), (2, half, NSA_GROUPS, HEAD_DIM)).reshape(2, half * KV_WIDTH)
    w2big = jnp.kron(eye, w2)
    return pe2, w1big.astype(BF16), w2big.astype(BF16)


def kernel(x, norm_mix, w_in, conv_w, nsa_q_norm, nsa_k_norm, cmp_pe_k, cmp_w1_k, cmp_w2_k,
           cmp_pe_v, cmp_w1_v, cmp_w2_v, w_out, norm_mlp, w_up, w_down):
    b, s, d = x.shape
    t = b * s
    depth = w_in.shape[0]
    ret_tabs = _retention_tables(s)
    ovt = _nsa_tables(s)
    bd128 = _block_diag_ones(KV_WIDTH, HEAD_DIM, 1.0 / HEAD_DIM).astype(BF16)
    bd256 = _block_diag_ones(RET_WIDTH, HEAD_DIM, 1.0 / HEAD_DIM).astype(BF16)
    c_rc = 4 * RET_WIDTH + 3 * CONV_WIDTH
    c_q = c_rc + NSA_WIDTH
    c_kc = c_q + KV_WIDTH
    c_vc = c_kc + KV_WIDTH
    c_kv = c_vc + 4 * KV_WIDTH
    n_gate = 3 * NSA_HEADS
    rows16 = s // CMP_STRIDE
    n_rc = RET_WIDTH + CONV_WIDTH

    xf = x.reshape(t, d)
    for l in range(depth):
        w = w_in[l].astype(BF16)
        w_all = jnp.concatenate([w[:, c_rc:c_kv], jnp.pad(w[:, c_kv:c_kv + n_gate], ((0, 0), (0, LANES - n_gate))),
                                 w[:, :c_rc]], axis=1)
        qgain = jnp.tile(nsa_q_norm[l], RET_WIDTH // HEAD_DIM)[None, :]
        kgain = jnp.tile(nsa_k_norm[l], NSA_GROUPS)[None, :]
        z_rc, qn, zkc, zvc, kv4, glog = _inproj(
            xf, norm_mix[l][None, :], w_all, qgain, jnp.tile(kgain, (1, 2)), bd256, c_rc, 4 * KV_WIDTH, LANES)
        y_rc = _retconv(z_rc.reshape(b, s, c_rc), *ret_tabs, bd256, conv_w[l])
        pek, w1k, w2k = _cmp_weights(cmp_pe_k[l], cmp_w1_k[l], cmp_w2_k[l])
        pev, w1v, w2v = _cmp_weights(cmp_pe_v[l], cmp_w1_v[l], cmp_w2_v[l])
        kcmp, vcmp = _cmp_prep(zkc.reshape(b, rows16, -1), zvc.reshape(b, rows16, -1),
                               pek, pev, w1k, w1v, w2k, w2v, kgain, bd128)
        y_nsa = _nsa(qn.reshape(b, s, NSA_HEADS * LANES), kcmp, vcmp, kv4.reshape(b, s, 4 * KV_WIDTH),
                     glog.reshape(b, s, LANES), ovt)
        wo = w_out[l].astype(BF16)
        xf = _outffn(xf, y_rc.reshape(t, n_rc), y_nsa.reshape(t, NSA_WIDTH), wo[:n_rc], wo[n_rc:],
                     norm_mlp[l][None, :], w_up[l].astype(BF16), w_down[l].astype(BF16))
    return xf.reshape(b, s, d)
```

```python
import functools

import numpy as np
import jax
import jax.numpy as jnp
from jax import lax
from jax.experimental import pallas as pl
from jax.experimental.pallas import tpu as pltpu

F32 = jnp.float32
BF16 = jnp.bfloat16

HEAD_DIM = 64
LANES = 128
RET_CHUNK = 128
RET_HEADS = 4
RET_WIDTH = RET_HEADS * HEAD_DIM
CONV_WIDTH = 256
NSA_HEADS = 8
NSA_GROUPS = 2
NSA_HPG = NSA_HEADS // NSA_GROUPS
NSA_WIDTH = NSA_HEADS * HEAD_DIM
KV_WIDTH = NSA_GROUPS * HEAD_DIM
CMP_LEN = 32
CMP_STRIDE = 16
SLC_LEN = 64
SLC_TOPK = 8
WIN_LEN = 256
Q_TILE = 256
SLC_CHUNK = 512
ROW_TILE = 512
FF_CHUNK = 1024
XPOS_BASE = 10000.0
EPS = 1e-6
NEG_INF = -1e30
FORCE_BONUS = 1e4
VMEM_LIMIT = 56 * 1024 * 1024


def _dot(a, b):
    return jnp.dot(a, b, preferred_element_type=F32)


def _dot_nt(a, b):
    return lax.dot_general(a, b, (((1,), (1,)), ((), ())), preferred_element_type=F32)


def _dot_tn(a, b):
    return lax.dot_general(a, b, (((0,), (0,)), ((), ())), preferred_element_type=F32)


def _split_bf16(x):
    hi = x.astype(BF16)
    lo = (x - hi.astype(F32)).astype(BF16)
    return hi, lo


def _group_rms(z, bd):
    return z * lax.rsqrt(_dot((z * z).astype(BF16), bd) + EPS)


def _inproj_kernel(x_ref, gain_ref, w_ref, qgain_ref, kgain_ref, bd_ref,
                   cos_ref, sin_ref, dmask_ref, qdec_ref, kdec_ref, sdec_ref, bdm_ref, convw_ref,
                   orc_ref, oq_ref, okc_ref, ovc_ref, okv_ref, og_ref, stk_ref, stv_ref, state_ref, carry_ref,
                   *, tiles_per_seq):
    @pl.when(pl.program_id(0) % tiles_per_seq == 0)
    def _():
        state_ref[...] = jnp.zeros_like(state_ref)
        carry_ref[...] = jnp.zeros_like(carry_ref)

    x = x_ref[...]
    h = x * lax.rsqrt(jnp.mean(x * x, axis=-1, keepdims=True) + EPS) * gain_ref[...]
    h = h.astype(BF16)
    bd = bd_ref[...]
    wide = bd.shape[0]
    z = _dot(h, w_ref[...])
    kw = KV_WIDTH
    c_kc = NSA_WIDTH
    c_kv = c_kc + 2 * kw
    c_g = c_kv + 4 * kw
    c_rc = c_g + og_ref.shape[1]
    zq = z[:, 0:NSA_WIDTH]
    qscale = qgain_ref[...] * (HEAD_DIM ** -0.5)
    lane_half = lax.broadcasted_iota(jnp.int32, (x.shape[0], LANES), 1) // HEAD_DIM
    for g in range(NSA_WIDTH // wide):
        qn = _group_rms(zq[:, g * wide:(g + 1) * wide], bd) * qscale
        for hh in range(NSA_HPG):
            blk = qn[:, (hh // 2) * LANES:(hh // 2 + 1) * LANES]
            if hh % 2 != g:
                blk = pltpu.roll(blk, HEAD_DIM, 1)
            hd = NSA_HPG * g + hh
            oq_ref[:, hd * LANES:(hd + 1) * LANES] = jnp.where(lane_half == g, blk, 0.0).astype(BF16)
    for c0, o_ref, st_ref in ((c_kc, okc_ref, stk_ref), (c_kc + kw, ovc_ref, stv_ref)):
        st_ref[...] = z[:, c0:c0 + kw]
        for r in range(CMP_STRIDE):
            o_ref[:, r * kw:(r + 1) * kw] = st_ref[pl.ds(r, x.shape[0] // CMP_STRIDE, stride=CMP_STRIDE), :].astype(BF16)
    zkv = z[:, c_kv:c_g]
    kn = _group_rms(jnp.concatenate([zkv[:, 0:kw], zkv[:, 2 * kw:3 * kw]], axis=1), bd) * kgain_ref[...]
    okv_ref[:, 0:kw] = kn[:, 0:kw].astype(BF16)
    okv_ref[:, kw:2 * kw] = zkv[:, kw:2 * kw].astype(BF16)
    okv_ref[:, 2 * kw:3 * kw] = kn[:, kw:2 * kw].astype(BF16)
    okv_ref[:, 3 * kw:4 * kw] = zkv[:, 3 * kw:4 * kw].astype(BF16)
    og_ref[...] = z[:, c_g:c_rc].astype(BF16)

    w = RET_WIDTH
    lane = lax.broadcasted_iota(jnp.int32, (RET_CHUNK, w), 1)
    even = lane % 2 == 0
    head = lane // HEAD_DIM

    def rotate(v, cos, sin):
        swapped = jnp.where(even, pltpu.roll(v, w - 1, 1), pltpu.roll(v, 1, 1))
        return v * cos + swapped * sin

    for cc in range(x.shape[0] // RET_CHUNK):
        rows = slice(cc * RET_CHUNK, (cc + 1) * RET_CHUNK)
        cos = cos_ref[rows, :]
        sin = sin_ref[rows, :]
        q = rotate(z[rows, c_rc:c_rc + w], cos, sin)
        k = rotate(z[rows, c_rc + w:c_rc + 2 * w], cos, sin) * (HEAD_DIM ** -0.5)
        vb = z[rows, c_rc + 2 * w:c_rc + 3 * w].astype(BF16)
        gate = z[rows, c_rc + 3 * w:c_rc + 4 * w]
        qb = q.astype(BF16)
        kb = k.astype(BF16)
        zero = jnp.zeros_like(qb)
        o = jnp.zeros((RET_CHUNK, w), F32)
        for hh in range(RET_HEADS):
            hm = head == hh
            scores = _dot_nt(jnp.where(hm, qb, zero), kb) * dmask_ref[hh]
            o = o + _dot(scores.astype(BF16), jnp.where(hm, vb, zero))
        state = state_ref[...]
        o = o + _dot((q * qdec_ref[...]).astype(BF16), state.astype(BF16))
        kv = _dot_tn((k * kdec_ref[...]).astype(BF16), vb)
        state_ref[...] = state * sdec_ref[...] + kv * bdm_ref[...]
        o = _group_rms(o, bd)
        orc_ref[rows, 0:w] = (gate * jax.nn.sigmoid(gate) * o).astype(BF16)

    c0 = c_rc + 4 * w
    cw = CONV_WIDTH
    u = z[:, c0 + cw:c0 + 2 * cw] * z[:, c0 + 2 * cw:c0 + 3 * cw]
    t_idx = lax.broadcasted_iota(jnp.int32, u.shape, 0)
    prev1 = carry_ref[7:8, :]
    prev2 = carry_ref[6:7, :]
    u1 = jnp.where(t_idx >= 1, pltpu.roll(u, 1, 0), prev1)
    u2 = jnp.where(t_idx >= 2, pltpu.roll(u, 2, 0), jnp.where(t_idx == 1, prev1, prev2))
    y = z[:, c0:c0 + cw] * (convw_ref[0:1, :] * u2 + convw_ref[1:2, :] * u1 + convw_ref[2:3, :] * u)
    orc_ref[:, w:w + cw] = y.astype(BF16)
    carry_ref[...] = u[u.shape[0] - 8:, :]


def _inproj(x2d, gain, w_all, qgain, kgain, bd, ret_tabs, convw, seq_len, n_kv, n_gate):
    t, d = x2d.shape
    tm = ROW_TILE
    tiles_per_seq = seq_len // tm
    cos, sin, dmask, qdec, kdec, sdec, bdm = ret_tabs
    n_rc = RET_WIDTH + CONV_WIDTH
    seq_rows = lambda a: pl.BlockSpec((tm, a.shape[1]), lambda i: (i % tiles_per_seq, 0))
    row = lambda w: pl.BlockSpec((tm, w), lambda i: (i, 0))
    full = lambda a: pl.BlockSpec(a.shape, lambda i: (0,) * a.ndim)
    wide16 = CMP_STRIDE * KV_WIDTH
    rows16 = pl.BlockSpec((tm // CMP_STRIDE, wide16), lambda i: (i, 0))
    sds16 = jax.ShapeDtypeStruct((t // CMP_STRIDE, wide16), BF16)
    widths = (n_rc, NSA_HEADS * LANES, None, None, n_kv, n_gate)
    return pl.pallas_call(
        functools.partial(_inproj_kernel, tiles_per_seq=tiles_per_seq),
        grid=(t // tm,),
        in_specs=[row(d)] + [full(a) for a in (gain, w_all, qgain, kgain, bd)] + [seq_rows(cos), seq_rows(sin)]
        + [full(a) for a in (dmask, qdec, kdec, sdec, bdm, convw)],
        out_specs=[rows16 if w is None else row(w) for w in widths],
        out_shape=[sds16 if w is None else jax.ShapeDtypeStruct((t, w), BF16) for w in widths],
        scratch_shapes=[pltpu.VMEM((tm, KV_WIDTH), F32)] * 2
        + [pltpu.VMEM((RET_WIDTH, RET_WIDTH), F32), pltpu.VMEM((8, CONV_WIDTH), F32)],
        compiler_params=pltpu.CompilerParams(dimension_semantics=("arbitrary",), vmem_limit_bytes=VMEM_LIMIT),
        name="inproj",
    )(x2d, gain, w_all, qgain, kgain, bd, cos, sin, dmask, qdec, kdec, sdec, bdm, convw)


def _cmp_kernel(xk_ref, xv_ref, pek_ref, pev_ref, w1k_ref, w1v_ref, w2k_ref, w2v_ref, kgain_ref, bd_ref,
                ok_ref, ov_ref):
    nrow = xk_ref.shape[1]
    row = lax.broadcasted_iota(jnp.int32, (nrow, KV_WIDTH), 0)

    def compress(x_ref, pe_ref, w1_ref, w2_ref):
        x = x_ref[0].astype(F32)
        first = _dot((x + pe_ref[0:1, :]).astype(BF16), w1_ref[0])
        second = _dot((x + pe_ref[1:2, :]).astype(BF16), w1_ref[1])
        pre = first + pltpu.roll(second, nrow - 1, 0)
        return _dot(jax.nn.silu(pre).astype(BF16), w2_ref[...])

    valid = row < nrow - 1
    kc = _group_rms(compress(xk_ref, pek_ref, w1k_ref, w2k_ref), bd_ref[...]) * kgain_ref[...]
    ok_ref[0] = jnp.where(valid, kc, 0.0).astype(BF16)
    vc = compress(xv_ref, pev_ref, w1v_ref, w2v_ref)
    ov_ref[0] = jnp.where(valid, vc, 0.0).astype(BF16)


def _cmp_prep(xk16, xv16, pek, pev, w1k, w1v, w2k, w2v, kgain, bd):
    b, nrow, wide = xk16.shape
    full = lambda a: pl.BlockSpec(a.shape, lambda i: (0,) * a.ndim)
    blk = pl.BlockSpec((1, nrow, wide), lambda i: (i, 0, 0))
    oblk = pl.BlockSpec((1, nrow, KV_WIDTH), lambda i: (i, 0, 0))
    return pl.pallas_call(
        _cmp_kernel,
        grid=(b,),
        in_specs=[blk, blk] + [full(a) for a in (pek, pev, w1k, w1v, w2k, w2v, kgain, bd)],
        out_specs=[oblk, oblk],
        out_shape=[jax.ShapeDtypeStruct((b, nrow, KV_WIDTH), BF16)] * 2,
        compiler_params=pltpu.CompilerParams(dimension_semantics=("arbitrary",), vmem_limit_bytes=VMEM_LIMIT),
        name="cmp_prep",
    )(xk16, xv16, pek, pev, w1k, w1v, w2k, w2v, kgain, bd)


def _nsa_kernel(q_ref, kc_ref, vc_ref, kv_ref, gl_ref, ovt_ref,
                out_ref, vtc_ref, vts_ref, vtw_ref, sel_ref, sc_ref, m_ref, acc_ref):
    i = pl.program_id(1)
    n_slc = ovt_ref.shape[0]
    tq = Q_TILE
    kc = SLC_CHUNK
    nb = kc // SLC_LEN
    s_len = kv_ref.shape[1]
    cols = NSA_HPG * tq
    groups = range(NSA_GROUPS)

    def transposed_v(v, g):
        vt = v.astype(F32).T
        other = lax.broadcasted_iota(jnp.int32, vt.shape, 0) // HEAD_DIM != g
        return jnp.where(other, 1.0, vt).astype(BF16)

    @pl.when(i == 0)
    def _():
        for g in groups:
            vtc_ref[g] = transposed_v(vc_ref[0], g)
            for c in range(s_len // kc):
                vts_ref[g, c] = transposed_v(kv_ref[0, c * kc:(c + 1) * kc, KV_WIDTH:2 * KV_WIDTH], g)
            for c in range(s_len // tq):
                vtw_ref[g, c] = transposed_v(kv_ref[0, c * tq:(c + 1) * tq, 3 * KV_WIDTH:4 * KV_WIDTH], g)

    def tile4(a):
        return jnp.concatenate([a] * NSA_HPG, axis=1)

    def colmax(a):
        return jnp.max(a, axis=0, keepdims=True)

    q = q_ref[0]
    t_q = i * tq + lax.broadcasted_iota(jnp.int32, (1, tq), 1)
    gates_t = jax.nn.sigmoid(gl_ref[0].astype(F32)).T
    q4s = [jnp.concatenate([q[:, (NSA_HPG * g + h) * LANES:(NSA_HPG * g + h + 1) * LANES] for h in range(NSA_HPG)],
                           axis=0) for g in groups]

    nwt = pl.cdiv(WIN_LEN, tq) + 1
    wtile = jnp.maximum(i - (nwt - 1), 0)
    kpos = wtile * tq + lax.broadcasted_iota(jnp.int32, (nwt * tq, tq), 0)
    wbias = tile4(jnp.where(kpos <= t_q, jnp.where(kpos > t_q - WIN_LEN, 0.0, NEG_INF), NEG_INF))
    wrows = pl.ds(pl.multiple_of(wtile * tq, tq), nwt * tq)
    cmp_end = lax.broadcasted_iota(jnp.int32, (kc_ref.shape[1], tq), 0) * CMP_STRIDE + (CMP_LEN - 1)
    cmp_bias = tile4(jnp.where(cmp_end <= t_q, 0.0, NEG_INF))
    cmp_any = tile4(jnp.where(t_q >= CMP_LEN - 1, 1.0, 0.0))

    o_cmps, o_wins = [], []
    for g in groups:
        rs = slice(g * HEAD_DIM, (g + 1) * HEAD_DIM)
        ls = (1 - g) * HEAD_DIM

        sw = _dot_nt(kv_ref[0, wrows, 2 * KV_WIDTH:3 * KV_WIDTH], q4s[g]) + wbias
        vtw = jnp.concatenate([vtw_ref[g, wtile + j] for j in range(nwt)], axis=1)
        ow = _dot(vtw, jnp.exp((sw - colmax(sw)).astype(BF16)))
        o_wins.append(ow[rs] * (1.0 / ow[ls:ls + 1]))

        sc = _dot_nt(kc_ref[0], q4s[g]) + cmp_bias
        e = jnp.exp(sc - colmax(sc))
        scale = cmp_any * (1.0 / jnp.maximum(jnp.sum(e, axis=0, keepdims=True), 1e-30))
        p = e * scale
        o_cmps.append(_dot(vtc_ref[g], p.astype(BF16))[rs])
        psum = p[:, 0:tq]
        for h in range(1, NSA_HPG):
            psum = psum + p[:, h * tq:(h + 1) * tq]
        p_hi, p_lo = _split_bf16(psum)
        imp = _dot(ovt_ref[...], p_hi) + _dot(ovt_ref[...], p_lo)

        j_idx = lax.broadcasted_iota(jnp.int32, (n_slc, tq), 0)
        t_blk = (i * tq + lax.broadcasted_iota(jnp.int32, (n_slc, tq), 1)) // SLC_LEN
        forced = (j_idx == 0) | (j_idx == t_blk) | (j_idx == t_blk - 1)
        score = jnp.where(j_idx <= t_blk, imp + jnp.where(forced, FORCE_BONUS, 0.0), -1.0)
        blocks = [score[8 * v:8 * v + 8] for v in range(n_slc // 8)]
        jrow = lax.broadcasted_iota(jnp.int32, (8, tq), 0)
        ranks = [jnp.zeros((8, tq), F32) for _ in blocks]
        for k in range(n_slc):
            sk = score[k:k + 1, :]
            for v, sb in enumerate(blocks):
                if 8 * v > k:
                    ahead = sk >= sb
                elif 8 * v + 7 < k:
                    ahead = sk > sb
                else:
                    ahead = jnp.where(jrow + 8 * v > k, jnp.where(sk >= sb, 1.0, 0.0), jnp.where(sk > sb, 1.0, 0.0)) > 0.5
                ranks[v] = ranks[v] + jnp.where(ahead, 1.0, 0.0)
        for v, r in enumerate(ranks):
            sel = jnp.where(r < float(SLC_TOPK), 0.0, NEG_INF)
            sel_ref[g, 8 * v:8 * v + 8] = jnp.broadcast_to(sel[:, None, :], (8, 8, tq))

    n_chunks = (i * tq + tq - 1) // kc + 1
    m_ref[...] = jnp.full(m_ref.shape, NEG_INF, F32)
    acc_ref[...] = jnp.zeros_like(acc_ref)
    krow = lax.broadcasted_iota(jnp.int32, (kc, tq), 0)

    def score_chunk(c):
        krows = kv_ref[0, pl.ds(pl.multiple_of(c * kc, kc), kc), 0:KV_WIDTH]
        causal = (c * kc + krow) <= t_q
        for g in groups:
            chosen = jnp.concatenate(
                [jnp.broadcast_to(sel_ref[g, c * nb + jb][None], (SLC_LEN // 8, 8, tq)).reshape(SLC_LEN, tq)
                 for jb in range(nb)], axis=0)
            sc = _dot_nt(krows, q4s[g]) + tile4(jnp.where(causal, chosen, NEG_INF))
            sc_ref[g, c] = sc
            m_ref[g] = jnp.maximum(m_ref[g], jnp.max(sc.reshape(kc // 8, 8, cols), axis=0))

    def accumulate_chunk(c, m_prev):
        m_out = []
        for g in groups:
            m_cur = colmax(m_ref[g])
            e = jnp.exp((sc_ref[g, c] - m_cur).astype(BF16))
            acc_ref[g] = acc_ref[g] * jnp.exp(m_prev[g] - m_cur) + _dot(vts_ref[g, c], e)
            m_out.append(m_cur)
        return tuple(m_out)

    def slc_trip(c, m_prev):
        m_cur = accumulate_chunk(c, m_prev)
        score_chunk(c + 1)
        return m_cur

    score_chunk(0)
    m_last = lax.fori_loop(0, n_chunks - 1, slc_trip, tuple(jnp.full((1, cols), NEG_INF, F32) for _ in groups))
    accumulate_chunk(n_chunks - 1, m_last)

    ys = []
    for g in groups:
        rs = slice(g * HEAD_DIM, (g + 1) * HEAD_DIM)
        ls = (1 - g) * HEAD_DIM
        acc = acc_ref[g]
        o_slc = acc[rs] * (1.0 / acc[ls:ls + 1])
        for h in range(NSA_HPG):
            r = (NSA_HPG * g + h) * 3
            blk = slice(h * tq, (h + 1) * tq)
            ys.append(gates_t[r:r + 1] * o_cmps[g][:, blk] + gates_t[r + 1:r + 2] * o_slc[:, blk]
                      + gates_t[r + 2:r + 3] * o_wins[g][:, blk])
    out_ref[0] = jnp.concatenate(ys, axis=0).T.astype(BF16)


def _nsa(qn, kcmp, vcmp, kv4, glog, ovt):
    b, s, _ = qn.shape
    nq = s // Q_TILE
    ncmp = kcmp.shape[1]
    n_slc = ovt.shape[0]
    cols = NSA_HPG * Q_TILE
    full = lambda a: pl.BlockSpec(a.shape, lambda bi, i: (0,) * a.ndim)
    return pl.pallas_call(
        _nsa_kernel,
        grid=(b, nq),
        in_specs=[pl.BlockSpec((1, Q_TILE, NSA_HEADS * LANES), lambda bi, i: (bi, i, 0)),
                  pl.BlockSpec((1, ncmp, KV_WIDTH), lambda bi, i: (bi, 0, 0)),
                  pl.BlockSpec((1, ncmp, KV_WIDTH), lambda bi, i: (bi, 0, 0)),
                  pl.BlockSpec((1, s, 4 * KV_WIDTH), lambda bi, i: (bi, 0, 0)),
                  pl.BlockSpec((1, Q_TILE, LANES), lambda bi, i: (bi, i, 0)),
                  full(ovt)],
        out_specs=pl.BlockSpec((1, Q_TILE, NSA_WIDTH), lambda bi, i: (bi, i, 0)),
        out_shape=jax.ShapeDtypeStruct((b, s, NSA_WIDTH), BF16),
        scratch_shapes=[pltpu.VMEM((NSA_GROUPS, KV_WIDTH, ncmp), BF16),
                        pltpu.VMEM((NSA_GROUPS, s // SLC_CHUNK, KV_WIDTH, SLC_CHUNK), BF16),
                        pltpu.VMEM((NSA_GROUPS, nq, KV_WIDTH, Q_TILE), BF16),
                        pltpu.VMEM((NSA_GROUPS, n_slc, 8, Q_TILE), F32),
                        pltpu.VMEM((NSA_GROUPS, s // SLC_CHUNK, SLC_CHUNK, cols), F32),
                        pltpu.VMEM((NSA_GROUPS, 8, cols), F32),
                        pltpu.VMEM((NSA_GROUPS, KV_WIDTH, cols), F32)],
        compiler_params=pltpu.CompilerParams(dimension_semantics=("arbitrary", "arbitrary"),
                                             vmem_limit_bytes=VMEM_LIMIT),
        name="nsa",
    )(qn, kcmp, vcmp, kv4, glog, ovt)


def _outffn_kernel(x_ref, yrc_ref, yn_ref, woa_ref, wob_ref, gain_ref, wup_ref, wdn_ref, out_ref):
    x1 = x_ref[...] + _dot(yrc_ref[...], woa_ref[...]) + _dot(yn_ref[...], wob_ref[...])
    h = x1 * lax.rsqrt(jnp.mean(x1 * x1, axis=-1, keepdims=True) + EPS) * gain_ref[...]
    h = h.astype(BF16)
    out_ref[...] = x1
    for c in range(wup_ref.shape[1] // FF_CHUNK):
        sl = slice(c * FF_CHUNK, (c + 1) * FF_CHUNK)
        u = jnp.maximum(_dot(h, wup_ref[:, sl]), 0.0)
        out_ref[...] += _dot((u * u).astype(BF16), wdn_ref[sl, :])


def _outffn(x2d, yrc, yn, woa, wob, gain, wup, wdn):
    t, d = x2d.shape
    tm = ROW_TILE
    row = lambda w: pl.BlockSpec((tm, w), lambda i: (i, 0))
    full = lambda a: pl.BlockSpec(a.shape, lambda i: (0,) * a.ndim)
    return pl.pallas_call(
        _outffn_kernel,
        grid=(t // tm,),
        in_specs=[row(d), row(yrc.shape[1]), row(yn.shape[1])] + [full(a) for a in (woa, wob, gain, wup, wdn)],
        out_specs=row(d),
        out_shape=jax.ShapeDtypeStruct((t, d), F32),
        compiler_params=pltpu.CompilerParams(dimension_semantics=("arbitrary",), vmem_limit_bytes=VMEM_LIMIT),
        name="outffn",
    )(x2d, yrc, yn, woa, wob, gain, wup, wdn)


def _block_diag_ones(n, block, value):
    idx = np.arange(n) // block
    return jnp.asarray((idx[:, None] == idx[None, :]).astype(np.float32) * value)


def _retention_tables(s):
    h, d, c = RET_HEADS, HEAD_DIM, RET_CHUNK
    pos = jnp.arange(s, dtype=F32)
    inv_freq = 1.0 / (XPOS_BASE ** jnp.linspace(0.0, 1.0, d // 2, dtype=F32))
    ang = jnp.repeat(pos[:, None] * inv_freq[None, :], 2, axis=-1)
    sign = jnp.tile(jnp.asarray([-1.0, 1.0], F32), d // 2)
    cos = jnp.tile(jnp.cos(ang), (1, h))
    sin = jnp.tile(jnp.sin(ang) * sign[None, :], (1, h))
    log_gamma = jnp.log(1.0 - 2.0 ** (-5.0 - jnp.arange(h, dtype=F32)))
    idx = jnp.arange(c, dtype=F32)
    rel = idx[:, None] - idx[None, :]
    dmask = jnp.where(rel >= 0, jnp.exp(log_gamma[:, None, None] * jnp.maximum(rel, 0.0)), 0.0)
    kdec = jnp.repeat(jnp.exp(log_gamma[:, None] * (c - 1.0 - idx)).T, d, axis=1)
    qdec = jnp.repeat(jnp.exp(log_gamma[:, None] * (idx + 1.0)).T, d, axis=1)
    bdm = _block_diag_ones(h * d, d, 1.0)
    sdec = bdm * jnp.repeat(jnp.exp(log_gamma * c), d)[:, None]
    return cos, sin, dmask, qdec, kdec, sdec, bdm


def _nsa_tables(s):
    n_rows = s // CMP_STRIDE
    n_cmp = (s - CMP_LEN) // CMP_STRIDE + 1
    n_slc = s // SLC_LEN
    cs = np.arange(n_cmp)[:, None] * CMP_STRIDE
    ss = np.arange(n_slc)[None, :] * SLC_LEN
    overlap = np.clip(np.minimum(cs + CMP_LEN, ss + SLC_LEN) - np.maximum(cs, ss), 0, None).astype(np.float32) / CMP_LEN
    ovt = np.zeros((n_slc, n_rows), np.float32)
    ovt[:, :n_cmp] = overlap.T
    return jnp.asarray(ovt, BF16)


def _cmp_weights(pe, w1, w2):
    half = CMP_LEN // 2
    eye = jnp.eye(NSA_GROUPS, dtype=F32)
    w1r = w1.reshape(2, half, HEAD_DIM, HEAD_DIM)
    w1big = jnp.einsum('arce,gh->argche', w1r, eye).reshape(2, half * KV_WIDTH, KV_WIDTH)
    pe2 = jnp.broadcast_to(pe.reshape(2, half, 1, HEAD_DIM), (2, half, NSA_GROUPS, HEAD_DIM)).reshape(2, half * KV_WIDTH)
    w2big = jnp.kron(eye, w2)
    return pe2, w1big.astype(BF16), w2big.astype(BF16)


def kernel(x, norm_mix, w_in, conv_w, nsa_q_norm, nsa_k_norm, cmp_pe_k, cmp_w1_k, cmp_w2_k,
           cmp_pe_v, cmp_w1_v, cmp_w2_v, w_out, norm_mlp, w_up, w_down):
    b, s, d = x.shape
    t = b * s
    depth = w_in.shape[0]
    ret_tabs = _retention_tables(s)
    ovt = _nsa_tables(s)
    bd128 = _block_diag_ones(KV_WIDTH, HEAD_DIM, 1.0 / HEAD_DIM).astype(BF16)
    bd256 = _block_diag_ones(RET_WIDTH, HEAD_DIM, 1.0 / HEAD_DIM).astype(BF16)
    c_rc = 4 * RET_WIDTH + 3 * CONV_WIDTH
    c_q = c_rc + NSA_WIDTH
    c_kc = c_q + KV_WIDTH
    c_vc = c_kc + KV_WIDTH
    c_kv = c_vc + 4 * KV_WIDTH
    n_gate = 3 * NSA_HEADS
    rows16 = s // CMP_STRIDE
    n_rc = RET_WIDTH + CONV_WIDTH

    xf = x.reshape(t, d)
    for l in range(depth):
        w = w_in[l].astype(BF16)
        w_all = jnp.concatenate([w[:, c_rc:c_kv], jnp.pad(w[:, c_kv:c_kv + n_gate], ((0, 0), (0, LANES - n_gate))),
                                 w[:, :c_rc]], axis=1)
        qgain = jnp.tile(nsa_q_norm[l], RET_WIDTH // HEAD_DIM)[None, :]
        kgain = jnp.tile(nsa_k_norm[l], NSA_GROUPS)[None, :]
        y_rc, qn, zkc, zvc, kv4, glog = _inproj(
            xf, norm_mix[l][None, :], w_all, qgain, jnp.tile(kgain, (1, 2)), bd256, ret_tabs, conv_w[l], s,
            4 * KV_WIDTH, LANES)
        pek, w1k, w2k = _cmp_weights(cmp_pe_k[l], cmp_w1_k[l], cmp_w2_k[l])
        pev, w1v, w2v = _cmp_weights(cmp_pe_v[l], cmp_w1_v[l], cmp_w2_v[l])
        kcmp, vcmp = _cmp_prep(zkc.reshape(b, rows16, -1), zvc.reshape(b, rows16, -1),
                               pek, pev, w1k, w1v, w2k, w2v, kgain, bd128)
        y_nsa = _nsa(qn.reshape(b, s, NSA_HEADS * LANES), kcmp, vcmp, kv4.reshape(b, s, 4 * KV_WIDTH),
                     glog.reshape(b, s, LANES), ovt)
        wo = w_out[l].astype(BF16)
        xf = _outffn(xf, y_rc.reshape(t, n_rc), y_nsa.reshape(t, NSA_WIDTH), wo[:n_rc], wo[n_rc:],
                     norm_mlp[l][None, :], w_up[l].astype(BF16), w_down[l].astype(BF16))
    return xf.reshape(b, s, d)
```

```python
import functools

import numpy as np
import jax
import jax.numpy as jnp
from jax import lax
from jax.experimental import pallas as pl
from jax.experimental.pallas import tpu as pltpu

F32 = jnp.float32
BF16 = jnp.bfloat16

HEAD_DIM = 64
LANES = 128
RET_CHUNK = 128
RET_HEADS = 4
RET_WIDTH = RET_HEADS * HEAD_DIM
CONV_WIDTH = 256
NSA_HEADS = 8
NSA_GROUPS = 2
NSA_HPG = NSA_HEADS // NSA_GROUPS
NSA_WIDTH = NSA_HEADS * HEAD_DIM
KV_WIDTH = NSA_GROUPS * HEAD_DIM
CMP_LEN = 32
CMP_STRIDE = 16
SLC_LEN = 64
SLC_TOPK = 8
WIN_LEN = 256
Q_TILE = 256
SLC_CHUNK = 512
ROW_TILE = 512
INPROJ_TILE = 1024
FF_CHUNK = 1024
XPOS_BASE = 10000.0
EPS = 1e-6
NEG_INF = -1e30
FORCE_BONUS = 1e4
VMEM_LIMIT = 56 * 1024 * 1024


def _dot(a, b):
    return jnp.dot(a, b, preferred_element_type=F32)


def _dot_nt(a, b):
    return lax.dot_general(a, b, (((1,), (1,)), ((), ())), preferred_element_type=F32)


def _dot_tn(a, b):
    return lax.dot_general(a, b, (((0,), (0,)), ((), ())), preferred_element_type=F32)


def _split_bf16(x):
    hi = x.astype(BF16)
    lo = (x - hi.astype(F32)).astype(BF16)
    return hi, lo


def _group_rms(z, bd):
    return z * lax.rsqrt(_dot((z * z).astype(BF16), bd) + EPS)


def _inproj_kernel(x_ref, gain_ref, w_ref, qgain_ref, kgain_ref, bd_ref,
                   cos_ref, sin_ref, dmask_ref, qdec_ref, kdec_ref, sdec_ref, bdm_ref, convw_ref,
                   orc_ref, oq_ref, okc_ref, ovc_ref, okv_ref, og_ref, stk_ref, stv_ref, state_ref, carry_ref,
                   *, tiles_per_seq):
    @pl.when(pl.program_id(0) % tiles_per_seq == 0)
    def _():
        state_ref[...] = jnp.zeros_like(state_ref)
        carry_ref[...] = jnp.zeros_like(carry_ref)

    x = x_ref[...]
    h = x * lax.rsqrt(jnp.mean(x * x, axis=-1, keepdims=True) + EPS) * gain_ref[...]
    h = h.astype(BF16)
    bd = bd_ref[...]
    wide = bd.shape[0]
    z = _dot(h, w_ref[...])
    kw = KV_WIDTH
    c_kc = NSA_WIDTH
    c_kv = c_kc + 2 * kw
    c_g = c_kv + 4 * kw
    c_rc = c_g + og_ref.shape[1]
    zq = z[:, 0:NSA_WIDTH]
    qscale = qgain_ref[...] * (HEAD_DIM ** -0.5)
    lane_half = lax.broadcasted_iota(jnp.int32, (x.shape[0], LANES), 1) // HEAD_DIM
    for g in range(NSA_WIDTH // wide):
        qn = _group_rms(zq[:, g * wide:(g + 1) * wide], bd) * qscale
        for hh in range(NSA_HPG):
            blk = qn[:, (hh // 2) * LANES:(hh // 2 + 1) * LANES]
            if hh % 2 != g:
                blk = pltpu.roll(blk, HEAD_DIM, 1)
            hd = NSA_HPG * g + hh
            oq_ref[:, hd * LANES:(hd + 1) * LANES] = jnp.where(lane_half == g, blk, 0.0).astype(BF16)
    for c0, o_ref, st_ref in ((c_kc, okc_ref, stk_ref), (c_kc + kw, ovc_ref, stv_ref)):
        st_ref[...] = z[:, c0:c0 + kw]
        for r in range(CMP_STRIDE):
            o_ref[:, r * kw:(r + 1) * kw] = st_ref[pl.ds(r, x.shape[0] // CMP_STRIDE, stride=CMP_STRIDE), :].astype(BF16)
    zkv = z[:, c_kv:c_g]
    kn = _group_rms(jnp.concatenate([zkv[:, 0:kw], zkv[:, 2 * kw:3 * kw]], axis=1), bd) * kgain_ref[...]
    okv_ref[:, 0:kw] = kn[:, 0:kw].astype(BF16)
    okv_ref[:, kw:2 * kw] = zkv[:, kw:2 * kw].astype(BF16)
    okv_ref[:, 2 * kw:3 * kw] = kn[:, kw:2 * kw].astype(BF16)
    okv_ref[:, 3 * kw:4 * kw] = zkv[:, 3 * kw:4 * kw].astype(BF16)
    og_ref[...] = z[:, c_g:c_rc].astype(BF16)

    w = RET_WIDTH
    lane = lax.broadcasted_iota(jnp.int32, (RET_CHUNK, w), 1)
    even = lane % 2 == 0
    head = lane // HEAD_DIM

    def rotate(v, cos, sin):
        swapped = jnp.where(even, pltpu.roll(v, w - 1, 1), pltpu.roll(v, 1, 1))
        return v * cos + swapped * sin

    for cc in range(x.shape[0] // RET_CHUNK):
        rows = slice(cc * RET_CHUNK, (cc + 1) * RET_CHUNK)
        cos = cos_ref[rows, :]
        sin = sin_ref[rows, :]
        q = rotate(z[rows, c_rc:c_rc + w], cos, sin)
        k = rotate(z[rows, c_rc + w:c_rc + 2 * w], cos, sin) * (HEAD_DIM ** -0.5)
        vb = z[rows, c_rc + 2 * w:c_rc + 3 * w].astype(BF16)
        gate = z[rows, c_rc + 3 * w:c_rc + 4 * w]
        qb = q.astype(BF16)
        kb = k.astype(BF16)
        zero = jnp.zeros_like(qb)
        o = jnp.zeros((RET_CHUNK, w), F32)
        for hh in range(RET_HEADS):
            hm = head == hh
            scores = _dot_nt(jnp.where(hm, qb, zero), kb) * dmask_ref[hh]
            o = o + _dot(scores.astype(BF16), jnp.where(hm, vb, zero))
        state = state_ref[...]
        o = o + _dot((q * qdec_ref[...]).astype(BF16), state.astype(BF16))
        kv = _dot_tn((k * kdec_ref[...]).astype(BF16), vb)
        state_ref[...] = state * sdec_ref[...] + kv * bdm_ref[...]
        o = _group_rms(o, bd)
        orc_ref[rows, 0:w] = (gate * jax.nn.sigmoid(gate) * o).astype(BF16)

    c0 = c_rc + 4 * w
    cw = CONV_WIDTH
    u = z[:, c0 + cw:c0 + 2 * cw] * z[:, c0 + 2 * cw:c0 + 3 * cw]
    t_idx = lax.broadcasted_iota(jnp.int32, u.shape, 0)
    prev1 = carry_ref[7:8, :]
    prev2 = carry_ref[6:7, :]
    u1 = jnp.where(t_idx >= 1, pltpu.roll(u, 1, 0), prev1)
    u2 = jnp.where(t_idx >= 2, pltpu.roll(u, 2, 0), jnp.where(t_idx == 1, prev1, prev2))
    y = z[:, c0:c0 + cw] * (convw_ref[0:1, :] * u2 + convw_ref[1:2, :] * u1 + convw_ref[2:3, :] * u)
    orc_ref[:, w:w + cw] = y.astype(BF16)
    carry_ref[...] = u[u.shape[0] - 8:, :]


def _inproj(x2d, gain, w_all, qgain, kgain, bd, ret_tabs, convw, seq_len, n_kv, n_gate):
    t, d = x2d.shape
    tm = INPROJ_TILE
    tiles_per_seq = seq_len // tm
    cos, sin, dmask, qdec, kdec, sdec, bdm = ret_tabs
    n_rc = RET_WIDTH + CONV_WIDTH
    seq_rows = lambda a: pl.BlockSpec((tm, a.shape[1]), lambda i: (i % tiles_per_seq, 0))
    row = lambda w: pl.BlockSpec((tm, w), lambda i: (i, 0))
    full = lambda a: pl.BlockSpec(a.shape, lambda i: (0,) * a.ndim)
    wide16 = CMP_STRIDE * KV_WIDTH
    rows16 = pl.BlockSpec((tm // CMP_STRIDE, wide16), lambda i: (i, 0))
    sds16 = jax.ShapeDtypeStruct((t // CMP_STRIDE, wide16), BF16)
    widths = (n_rc, NSA_HEADS * LANES, None, None, n_kv, n_gate)
    return pl.pallas_call(
        functools.partial(_inproj_kernel, tiles_per_seq=tiles_per_seq),
        grid=(t // tm,),
        in_specs=[row(d)] + [full(a) for a in (gain, w_all, qgain, kgain, bd)] + [seq_rows(cos), seq_rows(sin)]
        + [full(a) for a in (dmask, qdec, kdec, sdec, bdm, convw)],
        out_specs=[rows16 if w is None else row(w) for w in widths],
        out_shape=[sds16 if w is None else jax.ShapeDtypeStruct((t, w), BF16) for w in widths],
        scratch_shapes=[pltpu.VMEM((tm, KV_WIDTH), F32)] * 2
        + [pltpu.VMEM((RET_WIDTH, RET_WIDTH), F32), pltpu.VMEM((8, CONV_WIDTH), F32)],
        compiler_params=pltpu.CompilerParams(dimension_semantics=("arbitrary",), vmem_limit_bytes=VMEM_LIMIT),
        name="inproj",
    )(x2d, gain, w_all, qgain, kgain, bd, cos, sin, dmask, qdec, kdec, sdec, bdm, convw)


def _cmp_kernel(xk_ref, xv_ref, pek_ref, pev_ref, w1k_ref, w1v_ref, w2k_ref, w2v_ref, kgain_ref, bd_ref,
                ok_ref, ov_ref):
    nrow = xk_ref.shape[1]
    row = lax.broadcasted_iota(jnp.int32, (nrow, KV_WIDTH), 0)

    def compress(x_ref, pe_ref, w1_ref, w2_ref):
        x = x_ref[0].astype(F32)
        first = _dot((x + pe_ref[0:1, :]).astype(BF16), w1_ref[0])
        second = _dot((x + pe_ref[1:2, :]).astype(BF16), w1_ref[1])
        pre = first + pltpu.roll(second, nrow - 1, 0)
        return _dot(jax.nn.silu(pre).astype(BF16), w2_ref[...])

    valid = row < nrow - 1
    kc = _group_rms(compress(xk_ref, pek_ref, w1k_ref, w2k_ref), bd_ref[...]) * kgain_ref[...]
    ok_ref[0] = jnp.where(valid, kc, 0.0).astype(BF16)
    vc = compress(xv_ref, pev_ref, w1v_ref, w2v_ref)
    ov_ref[0] = jnp.where(valid, vc, 0.0).astype(BF16)


def _cmp_prep(xk16, xv16, pek, pev, w1k, w1v, w2k, w2v, kgain, bd):
    b, nrow, wide = xk16.shape
    full = lambda a: pl.BlockSpec(a.shape, lambda i: (0,) * a.ndim)
    blk = pl.BlockSpec((1, nrow, wide), lambda i: (i, 0, 0))
    oblk = pl.BlockSpec((1, nrow, KV_WIDTH), lambda i: (i, 0, 0))
    return pl.pallas_call(
        _cmp_kernel,
        grid=(b,),
        in_specs=[blk, blk] + [full(a) for a in (pek, pev, w1k, w1v, w2k, w2v, kgain, bd)],
        out_specs=[oblk, oblk],
        out_shape=[jax.ShapeDtypeStruct((b, nrow, KV_WIDTH), BF16)] * 2,
        compiler_params=pltpu.CompilerParams(dimension_semantics=("arbitrary",), vmem_limit_bytes=VMEM_LIMIT),
        name="cmp_prep",
    )(xk16, xv16, pek, pev, w1k, w1v, w2k, w2v, kgain, bd)


def _nsa_kernel(q_ref, kc_ref, vc_ref, kv_ref, gl_ref, ovt_ref,
                out_ref, vtc_ref, vts_ref, vtw_ref, sel_ref, sc_ref, m_ref, acc_ref):
    i = pl.program_id(1)
    n_slc = ovt_ref.shape[0]
    tq = Q_TILE
    kc = SLC_CHUNK
    nb = kc // SLC_LEN
    s_len = kv_ref.shape[1]
    cols = NSA_HPG * tq
    groups = range(NSA_GROUPS)

    def transposed_v(v, g):
        vt = v.astype(F32).T
        other = lax.broadcasted_iota(jnp.int32, vt.shape, 0) // HEAD_DIM != g
        return jnp.where(other, 1.0, vt).astype(BF16)

    @pl.when(i == 0)
    def _():
        for g in groups:
            vtc_ref[g] = transposed_v(vc_ref[0], g)
            for c in range(s_len // kc):
                vts_ref[g, c] = transposed_v(kv_ref[0, c * kc:(c + 1) * kc, KV_WIDTH:2 * KV_WIDTH], g)
            for c in range(s_len // tq):
                vtw_ref[g, c] = transposed_v(kv_ref[0, c * tq:(c + 1) * tq, 3 * KV_WIDTH:4 * KV_WIDTH], g)

    def tile4(a):
        return jnp.concatenate([a] * NSA_HPG, axis=1)

    def colmax(a):
        return jnp.max(a, axis=0, keepdims=True)

    q = q_ref[0]
    t_q = i * tq + lax.broadcasted_iota(jnp.int32, (1, tq), 1)
    gates_t = jax.nn.sigmoid(gl_ref[0].astype(F32)).T
    q4s = [jnp.concatenate([q[:, (NSA_HPG * g + h) * LANES:(NSA_HPG * g + h + 1) * LANES] for h in range(NSA_HPG)],
                           axis=0) for g in groups]

    nwt = pl.cdiv(WIN_LEN, tq) + 1
    wtile = jnp.maximum(i - (nwt - 1), 0)
    kpos = wtile * tq + lax.broadcasted_iota(jnp.int32, (nwt * tq, tq), 0)
    wbias = tile4(jnp.where(kpos <= t_q, jnp.where(kpos > t_q - WIN_LEN, 0.0, NEG_INF), NEG_INF))
    wrows = pl.ds(pl.multiple_of(wtile * tq, tq), nwt * tq)
    cmp_end = lax.broadcasted_iota(jnp.int32, (kc_ref.shape[1], tq), 0) * CMP_STRIDE + (CMP_LEN - 1)
    cmp_bias = tile4(jnp.where(cmp_end <= t_q, 0.0, NEG_INF))
    cmp_any = tile4(jnp.where(t_q >= CMP_LEN - 1, 1.0, 0.0))

    o_cmps, o_wins = [], []
    for g in groups:
        rs = slice(g * HEAD_DIM, (g + 1) * HEAD_DIM)
        ls = (1 - g) * HEAD_DIM

        sw = _dot_nt(kv_ref[0, wrows, 2 * KV_WIDTH:3 * KV_WIDTH], q4s[g]) + wbias
        vtw = jnp.concatenate([vtw_ref[g, wtile + j] for j in range(nwt)], axis=1)
        ow = _dot(vtw, jnp.exp((sw - colmax(sw)).astype(BF16)))
        o_wins.append(ow[rs] * (1.0 / ow[ls:ls + 1]))

        sc = _dot_nt(kc_ref[0], q4s[g]) + cmp_bias
        e = jnp.exp(sc - colmax(sc))
        scale = cmp_any * (1.0 / jnp.maximum(jnp.sum(e, axis=0, keepdims=True), 1e-30))
        p = e * scale
        o_cmps.append(_dot(vtc_ref[g], p.astype(BF16))[rs])
        psum = p[:, 0:tq]
        for h in range(1, NSA_HPG):
            psum = psum + p[:, h * tq:(h + 1) * tq]
        p_hi, p_lo = _split_bf16(psum)
        imp = _dot(ovt_ref[...], p_hi) + _dot(ovt_ref[...], p_lo)

        j_idx = lax.broadcasted_iota(jnp.int32, (n_slc, tq), 0)
        t_blk = (i * tq + lax.broadcasted_iota(jnp.int32, (n_slc, tq), 1)) // SLC_LEN
        forced = (j_idx == 0) | (j_idx == t_blk) | (j_idx == t_blk - 1)
        score = jnp.where(j_idx <= t_blk, imp + jnp.where(forced, FORCE_BONUS, 0.0), -1.0)
        blocks = [score[8 * v:8 * v + 8] for v in range(n_slc // 8)]
        jrow = lax.broadcasted_iota(jnp.int32, (8, tq), 0)
        ranks = [jnp.zeros((8, tq), F32) for _ in blocks]
        for k in range(n_slc):
            sk = score[k:k + 1, :]
            for v, sb in enumerate(blocks):
                if 8 * v > k:
                    ahead = sk >= sb
                elif 8 * v + 7 < k:
                    ahead = sk > sb
                else:
                    ahead = jnp.where(jrow + 8 * v > k, jnp.where(sk >= sb, 1.0, 0.0), jnp.where(sk > sb, 1.0, 0.0)) > 0.5
                ranks[v] = ranks[v] + jnp.where(ahead, 1.0, 0.0)
        for v, r in enumerate(ranks):
            sel = jnp.where(r < float(SLC_TOPK), 0.0, NEG_INF)
            sel_ref[g, 8 * v:8 * v + 8] = jnp.broadcast_to(sel[:, None, :], (8, 8, tq))

    n_chunks = (i * tq + tq - 1) // kc + 1
    m_ref[...] = jnp.full(m_ref.shape, NEG_INF, F32)
    acc_ref[...] = jnp.zeros_like(acc_ref)
    krow = lax.broadcasted_iota(jnp.int32, (kc, tq), 0)

    def score_chunk(c):
        krows = kv_ref[0, pl.ds(pl.multiple_of(c * kc, kc), kc), 0:KV_WIDTH]
        causal = (c * kc + krow) <= t_q
        for g in groups:
            chosen = jnp.concatenate(
                [jnp.broadcast_to(sel_ref[g, c * nb + jb][None], (SLC_LEN // 8, 8, tq)).reshape(SLC_LEN, tq)
                 for jb in range(nb)], axis=0)
            sc = _dot_nt(krows, q4s[g]) + tile4(jnp.where(causal, chosen, NEG_INF))
            sc_ref[g, c] = sc
            m_ref[g] = jnp.maximum(m_ref[g], jnp.max(sc.reshape(kc // 8, 8, cols), axis=0))

    def accumulate_chunk(c, m_prev):
        m_out = []
        for g in groups:
            m_cur = colmax(m_ref[g])
            e = jnp.exp((sc_ref[g, c] - m_cur).astype(BF16))
            acc_ref[g] = acc_ref[g] * jnp.exp(m_prev[g] - m_cur) + _dot(vts_ref[g, c], e)
            m_out.append(m_cur)
        return tuple(m_out)

    def slc_trip(c, m_prev):
        m_cur = accumulate_chunk(c, m_prev)
        score_chunk(c + 1)
        return m_cur

    score_chunk(0)
    m_last = lax.fori_loop(0, n_chunks - 1, slc_trip, tuple(jnp.full((1, cols), NEG_INF, F32) for _ in groups))
    accumulate_chunk(n_chunks - 1, m_last)

    ys = []
    for g in groups:
        rs = slice(g * HEAD_DIM, (g + 1) * HEAD_DIM)
        ls = (1 - g) * HEAD_DIM
        acc = acc_ref[g]
        o_slc = acc[rs] * (1.0 / acc[ls:ls + 1])
        for h in range(NSA_HPG):
            r = (NSA_HPG * g + h) * 3
            blk = slice(h * tq, (h + 1) * tq)
            ys.append(gates_t[r:r + 1] * o_cmps[g][:, blk] + gates_t[r + 1:r + 2] * o_slc[:, blk]
                      + gates_t[r + 2:r + 3] * o_wins[g][:, blk])
    out_ref[0] = jnp.concatenate(ys, axis=0).T.astype(BF16)


def _nsa(qn, kcmp, vcmp, kv4, glog, ovt):
    b, s, _ = qn.shape
    nq = s // Q_TILE
    ncmp = kcmp.shape[1]
    n_slc = ovt.shape[0]
    cols = NSA_HPG * Q_TILE
    full = lambda a: pl.BlockSpec(a.shape, lambda bi, i: (0,) * a.ndim)
    return pl.pallas_call(
        _nsa_kernel,
        grid=(b, nq),
        in_specs=[pl.BlockSpec((1, Q_TILE, NSA_HEADS * LANES), lambda bi, i: (bi, i, 0)),
                  pl.BlockSpec((1, ncmp, KV_WIDTH), lambda bi, i: (bi, 0, 0)),
                  pl.BlockSpec((1, ncmp, KV_WIDTH), lambda bi, i: (bi, 0, 0)),
                  pl.BlockSpec((1, s, 4 * KV_WIDTH), lambda bi, i: (bi, 0, 0)),
                  pl.BlockSpec((1, Q_TILE, LANES), lambda bi, i: (bi, i, 0)),
                  full(ovt)],
        out_specs=pl.BlockSpec((1, Q_TILE, NSA_WIDTH), lambda bi, i: (bi, i, 0)),
        out_shape=jax.ShapeDtypeStruct((b, s, NSA_WIDTH), BF16),
        scratch_shapes=[pltpu.VMEM((NSA_GROUPS, KV_WIDTH, ncmp), BF16),
                        pltpu.VMEM((NSA_GROUPS, s // SLC_CHUNK, KV_WIDTH, SLC_CHUNK), BF16),
                        pltpu.VMEM((NSA_GROUPS, nq, KV_WIDTH, Q_TILE), BF16),
                        pltpu.VMEM((NSA_GROUPS, n_slc, 8, Q_TILE), F32),
                        pltpu.VMEM((NSA_GROUPS, s // SLC_CHUNK, SLC_CHUNK, cols), F32),
                        pltpu.VMEM((NSA_GROUPS, 8, cols), F32),
                        pltpu.VMEM((NSA_GROUPS, KV_WIDTH, cols), F32)],
        compiler_params=pltpu.CompilerParams(dimension_semantics=("arbitrary", "arbitrary"),
                                             vmem_limit_bytes=VMEM_LIMIT),
        name="nsa",
    )(qn, kcmp, vcmp, kv4, glog, ovt)


def _outffn_kernel(x_ref, yrc_ref, yn_ref, woa_ref, wob_ref, gain_ref, wup_ref, wdn_ref, out_ref):
    x1 = x_ref[...] + _dot(yrc_ref[...], woa_ref[...]) + _dot(yn_ref[...], wob_ref[...])
    h = x1 * lax.rsqrt(jnp.mean(x1 * x1, axis=-1, keepdims=True) + EPS) * gain_ref[...]
    h = h.astype(BF16)
    out_ref[...] = x1
    for c in range(wup_ref.shape[1] // FF_CHUNK):
        sl = slice(c * FF_CHUNK, (c + 1) * FF_CHUNK)
        u = jnp.maximum(_dot(h, wup_ref[:, sl]), 0.0)
        out_ref[...] += _dot((u * u).astype(BF16), wdn_ref[sl, :])


def _outffn(x2d, yrc, yn, woa, wob, gain, wup, wdn):
    t, d = x2d.shape
    tm = ROW_TILE
    row = lambda w: pl.BlockSpec((tm, w), lambda i: (i, 0))
    full = lambda a: pl.BlockSpec(a.shape, lambda i: (0,) * a.ndim)
    return pl.pallas_call(
        _outffn_kernel,
        grid=(t // tm,),
        in_specs=[row(d), row(yrc.shape[1]), row(yn.shape[1])] + [full(a) for a in (woa, wob, gain, wup, wdn)],
        out_specs=row(d),
        out_shape=jax.ShapeDtypeStruct((t, d), F32),
        compiler_params=pltpu.CompilerParams(dimension_semantics=("arbitrary",), vmem_limit_bytes=VMEM_LIMIT),
        name="outffn",
    )(x2d, yrc, yn, woa, wob, gain, wup, wdn)


def _block_diag_ones(n, block, value):
    idx = np.arange(n) // block
    return jnp.asarray((idx[:, None] == idx[None, :]).astype(np.float32) * value)


def _retention_tables(s):
    h, d, c = RET_HEADS, HEAD_DIM, RET_CHUNK
    pos = jnp.arange(s, dtype=F32)
    inv_freq = 1.0 / (XPOS_BASE ** jnp.linspace(0.0, 1.0, d // 2, dtype=F32))
    ang = jnp.repeat(pos[:, None] * inv_freq[None, :], 2, axis=-1)
    sign = jnp.tile(jnp.asarray([-1.0, 1.0], F32), d // 2)
    cos = jnp.tile(jnp.cos(ang), (1, h))
    sin = jnp.tile(jnp.sin(ang) * sign[None, :], (1, h))
    log_gamma = jnp.log(1.0 - 2.0 ** (-5.0 - jnp.arange(h, dtype=F32)))
    idx = jnp.arange(c, dtype=F32)
    rel = idx[:, None] - idx[None, :]
    dmask = jnp.where(rel >= 0, jnp.exp(log_gamma[:, None, None] * jnp.maximum(rel, 0.0)), 0.0)
    kdec = jnp.repeat(jnp.exp(log_gamma[:, None] * (c - 1.0 - idx)).T, d, axis=1)
    qdec = jnp.repeat(jnp.exp(log_gamma[:, None] * (idx + 1.0)).T, d, axis=1)
    bdm = _block_diag_ones(h * d, d, 1.0)
    sdec = bdm * jnp.repeat(jnp.exp(log_gamma * c), d)[:, None]
    return cos, sin, dmask, qdec, kdec, sdec, bdm


def _nsa_tables(s):
    n_rows = s // CMP_STRIDE
    n_cmp = (s - CMP_LEN) // CMP_STRIDE + 1
    n_slc = s // SLC_LEN
    cs = np.arange(n_cmp)[:, None] * CMP_STRIDE
    ss = np.arange(n_slc)[None, :] * SLC_LEN
    overlap = np.clip(np.minimum(cs + CMP_LEN, ss + SLC_LEN) - np.maximum(cs, ss), 0, None).astype(np.float32) / CMP_LEN
    ovt = np.zeros((n_slc, n_rows), np.float32)
    ovt[:, :n_cmp] = overlap.T
    return jnp.asarray(ovt, BF16)


def _cmp_weights(pe, w1, w2):
    half = CMP_LEN // 2
    eye = jnp.eye(NSA_GROUPS, dtype=F32)
    w1r = w1.reshape(2, half, HEAD_DIM, HEAD_DIM)
    w1big = jnp.einsum('arce,gh->argche', w1r, eye).reshape(2, half * KV_WIDTH, KV_WIDTH)
    pe2 = jnp.broadcast_to(pe.reshape(2, half, 1, HEAD_DIM), (2, half, NSA_GROUPS, HEAD_DIM)).reshape(2, half * KV_WIDTH)
    w2big = jnp.kron(eye, w2)
    return pe2, w1big.astype(BF16), w2big.astype(BF16)


def kernel(x, norm_mix, w_in, conv_w, nsa_q_norm, nsa_k_norm, cmp_pe_k, cmp_w1_k, cmp_w2_k,
           cmp_pe_v, cmp_w1_v, cmp_w2_v, w_out, norm_mlp, w_up, w_down):
    b, s, d = x.shape
    t = b * s
    depth = w_in.shape[0]
    ret_tabs = _retention_tables(s)
    ovt = _nsa_tables(s)
    bd128 = _block_diag_ones(KV_WIDTH, HEAD_DIM, 1.0 / HEAD_DIM).astype(BF16)
    bd256 = _block_diag_ones(RET_WIDTH, HEAD_DIM, 1.0 / HEAD_DIM).astype(BF16)
    c_rc = 4 * RET_WIDTH + 3 * CONV_WIDTH
    c_q = c_rc + NSA_WIDTH
    c_kc = c_q + KV_WIDTH
    c_vc = c_kc + KV_WIDTH
    c_kv = c_vc + 4 * KV_WIDTH
    n_gate = 3 * NSA_HEADS
    rows16 = s // CMP_STRIDE
    n_rc = RET_WIDTH + CONV_WIDTH

    xf = x.reshape(t, d)
    for l in range(depth):
        w = w_in[l].astype(BF16)
        w_all = jnp.concatenate([w[:, c_rc:c_kv], jnp.pad(w[:, c_kv:c_kv + n_gate], ((0, 0), (0, LANES - n_gate))),
                                 w[:, :c_rc]], axis=1)
        qgain = jnp.tile(nsa_q_norm[l], RET_WIDTH // HEAD_DIM)[None, :]
        kgain = jnp.tile(nsa_k_norm[l], NSA_GROUPS)[None, :]
        y_rc, qn, zkc, zvc, kv4, glog = _inproj(
            xf, norm_mix[l][None, :], w_all, qgain, jnp.tile(kgain, (1, 2)), bd256, ret_tabs, conv_w[l], s,
            4 * KV_WIDTH, LANES)
        pek, w1k, w2k = _cmp_weights(cmp_pe_k[l], cmp_w1_k[l], cmp_w2_k[l])
        pev, w1v, w2v = _cmp_weights(cmp_pe_v[l], cmp_w1_v[l], cmp_w2_v[l])
        kcmp, vcmp = _cmp_prep(zkc.reshape(b, rows16, -1), zvc.reshape(b, rows16, -1),
                               pek, pev, w1k, w1v, w2k, w2v, kgain, bd128)
        y_nsa = _nsa(qn.reshape(b, s, NSA_HEADS * LANES), kcmp, vcmp, kv4.reshape(b, s, 4 * KV_WIDTH),
                     glog.reshape(b, s, LANES), ovt)
        wo = w_out[l].astype(BF16)
        xf = _outffn(xf, y_rc.reshape(t, n_rc), y_nsa.reshape(t, NSA_WIDTH), wo[:n_rc], wo[n_rc:],
                     norm_mlp[l][None, :], w_up[l].astype(BF16), w_down[l].astype(BF16))
    return xf.reshape(b, s, d)
```

```python
import functools

import numpy as np
import jax
import jax.numpy as jnp
from jax import lax
from jax.experimental import pallas as pl
from jax.experimental.pallas import tpu as pltpu

F32 = jnp.float32
BF16 = jnp.bfloat16

HEAD_DIM = 64
LANES = 128
RET_CHUNK = 128
RET_HEADS = 4
RET_WIDTH = RET_HEADS * HEAD_DIM
CONV_WIDTH = 256
NSA_HEADS = 8
NSA_GROUPS = 2
NSA_HPG = NSA_HEADS // NSA_GROUPS
NSA_WIDTH = NSA_HEADS * HEAD_DIM
KV_WIDTH = NSA_GROUPS * HEAD_DIM
CMP_LEN = 32
CMP_STRIDE = 16
SLC_LEN = 64
SLC_TOPK = 8
WIN_LEN = 256
Q_TILE = 256
SLC_CHUNK = 512
ROW_TILE = 512
INPROJ_TILE = 1024
FF_CHUNK = 1024
XPOS_BASE = 10000.0
EPS = 1e-6
NEG_INF = -1e30
FORCE_BONUS = 1e4
VMEM_LIMIT = 56 * 1024 * 1024


def _dot(a, b):
    return jnp.dot(a, b, preferred_element_type=F32)


def _dot_nt(a, b):
    return lax.dot_general(a, b, (((1,), (1,)), ((), ())), preferred_element_type=F32)


def _dot_tn(a, b):
    return lax.dot_general(a, b, (((0,), (0,)), ((), ())), preferred_element_type=F32)


def _split_bf16(x):
    hi = x.astype(BF16)
    lo = (x - hi.astype(F32)).astype(BF16)
    return hi, lo


def _group_rms(z, bd):
    return z * lax.rsqrt(_dot((z * z).astype(BF16), bd) + EPS)


def _inproj_kernel(x_ref, gain_ref, w_ref, qgain_ref, kgain_ref, bd_ref,
                   cos_ref, sin_ref, dmask_ref, qdec_ref, kdec_ref, sdec_ref, bdm_ref, convw_ref,
                   orc_ref, oq_ref, okc_ref, ovc_ref, okv_ref, og_ref, stk_ref, stv_ref, state_ref, carry_ref,
                   *, tiles_per_seq):
    @pl.when(pl.program_id(0) % tiles_per_seq == 0)
    def _():
        state_ref[...] = jnp.zeros_like(state_ref)
        carry_ref[...] = jnp.zeros_like(carry_ref)

    x = x_ref[...]
    h = x * lax.rsqrt(jnp.mean(x * x, axis=-1, keepdims=True) + EPS) * gain_ref[...]
    h = h.astype(BF16)
    bd = bd_ref[...]
    wide = bd.shape[0]
    z = _dot(h, w_ref[...])
    kw = KV_WIDTH
    c_kc = NSA_WIDTH
    c_kv = c_kc + 2 * kw
    c_g = c_kv + 4 * kw
    c_rc = c_g + og_ref.shape[1]
    zq = z[:, 0:NSA_WIDTH]
    qscale = qgain_ref[...] * (HEAD_DIM ** -0.5)
    lane_half = lax.broadcasted_iota(jnp.int32, (x.shape[0], LANES), 1) // HEAD_DIM
    for g in range(NSA_WIDTH // wide):
        qn = _group_rms(zq[:, g * wide:(g + 1) * wide], bd) * qscale
        for hh in range(NSA_HPG):
            blk = qn[:, (hh // 2) * LANES:(hh // 2 + 1) * LANES]
            if hh % 2 != g:
                blk = pltpu.roll(blk, HEAD_DIM, 1)
            hd = NSA_HPG * g + hh
            oq_ref[:, hd * LANES:(hd + 1) * LANES] = jnp.where(lane_half == g, blk, 0.0).astype(BF16)
    for c0, o_ref, st_ref in ((c_kc, okc_ref, stk_ref), (c_kc + kw, ovc_ref, stv_ref)):
        st_ref[...] = z[:, c0:c0 + kw]
        for r in range(CMP_STRIDE):
            o_ref[:, r * kw:(r + 1) * kw] = st_ref[pl.ds(r, x.shape[0] // CMP_STRIDE, stride=CMP_STRIDE), :].astype(BF16)
    zkv = z[:, c_kv:c_g]
    kn = _group_rms(jnp.concatenate([zkv[:, 0:kw], zkv[:, 2 * kw:3 * kw]], axis=1), bd) * kgain_ref[...]
    okv_ref[:, 0:kw] = kn[:, 0:kw].astype(BF16)
    okv_ref[:, kw:2 * kw] = zkv[:, kw:2 * kw].astype(BF16)
    okv_ref[:, 2 * kw:3 * kw] = kn[:, kw:2 * kw].astype(BF16)
    okv_ref[:, 3 * kw:4 * kw] = zkv[:, 3 * kw:4 * kw].astype(BF16)
    og_ref[...] = z[:, c_g:c_rc].astype(BF16)

    w = RET_WIDTH
    lane = lax.broadcasted_iota(jnp.int32, (RET_CHUNK, w), 1)
    even = lane % 2 == 0
    head = lane // HEAD_DIM

    def rotate(v, cos, sin):
        swapped = jnp.where(even, pltpu.roll(v, w - 1, 1), pltpu.roll(v, 1, 1))
        return v * cos + swapped * sin

    for cc in range(x.shape[0] // RET_CHUNK):
        rows = slice(cc * RET_CHUNK, (cc + 1) * RET_CHUNK)
        cos = cos_ref[rows, :]
        sin = sin_ref[rows, :]
        q = rotate(z[rows, c_rc:c_rc + w], cos, sin)
        k = rotate(z[rows, c_rc + w:c_rc + 2 * w], cos, sin) * (HEAD_DIM ** -0.5)
        vb = z[rows, c_rc + 2 * w:c_rc + 3 * w].astype(BF16)
        gate = z[rows, c_rc + 3 * w:c_rc + 4 * w]
        qb = q.astype(BF16)
        kb = k.astype(BF16)
        zero = jnp.zeros_like(qb)
        o = jnp.zeros((RET_CHUNK, w), F32)
        for hh in range(RET_HEADS):
            hm = head == hh
            scores = _dot_nt(jnp.where(hm, qb, zero), kb) * dmask_ref[hh]
            o = o + _dot(scores.astype(BF16), jnp.where(hm, vb, zero))
        state = state_ref[...]
        o = o + _dot((q * qdec_ref[...]).astype(BF16), state.astype(BF16))
        kv = _dot_tn((k * kdec_ref[...]).astype(BF16), vb)
        state_ref[...] = state * sdec_ref[...] + kv * bdm_ref[...]
        o = _group_rms(o, bd)
        orc_ref[rows, 0:w] = (gate * jax.nn.sigmoid(gate) * o).astype(BF16)

    c0 = c_rc + 4 * w
    cw = CONV_WIDTH
    u = z[:, c0 + cw:c0 + 2 * cw] * z[:, c0 + 2 * cw:c0 + 3 * cw]
    t_idx = lax.broadcasted_iota(jnp.int32, u.shape, 0)
    prev1 = carry_ref[7:8, :]
    prev2 = carry_ref[6:7, :]
    u1 = jnp.where(t_idx >= 1, pltpu.roll(u, 1, 0), prev1)
    u2 = jnp.where(t_idx >= 2, pltpu.roll(u, 2, 0), jnp.where(t_idx == 1, prev1, prev2))
    y = z[:, c0:c0 + cw] * (convw_ref[0:1, :] * u2 + convw_ref[1:2, :] * u1 + convw_ref[2:3, :] * u)
    orc_ref[:, w:w + cw] = y.astype(BF16)
    carry_ref[...] = u[u.shape[0] - 8:, :]


def _inproj(x2d, gain, w_all, qgain, kgain, bd, ret_tabs, convw, seq_len, n_kv, n_gate):
    t, d = x2d.shape
    tm = INPROJ_TILE
    tiles_per_seq = seq_len // tm
    cos, sin, dmask, qdec, kdec, sdec, bdm = ret_tabs
    n_rc = RET_WIDTH + CONV_WIDTH
    seq_rows = lambda a: pl.BlockSpec((tm, a.shape[1]), lambda i: (i % tiles_per_seq, 0))
    row = lambda w: pl.BlockSpec((tm, w), lambda i: (i, 0))
    full = lambda a: pl.BlockSpec(a.shape, lambda i: (0,) * a.ndim)
    wide16 = CMP_STRIDE * KV_WIDTH
    rows16 = pl.BlockSpec((tm // CMP_STRIDE, wide16), lambda i: (i, 0))
    sds16 = jax.ShapeDtypeStruct((t // CMP_STRIDE, wide16), BF16)
    widths = (n_rc, NSA_HEADS * LANES, None, None, n_kv, n_gate)
    return pl.pallas_call(
        functools.partial(_inproj_kernel, tiles_per_seq=tiles_per_seq),
        grid=(t // tm,),
        in_specs=[row(d)] + [full(a) for a in (gain, w_all, qgain, kgain, bd)] + [seq_rows(cos), seq_rows(sin)]
        + [full(a) for a in (dmask, qdec, kdec, sdec, bdm, convw)],
        out_specs=[rows16 if w is None else row(w) for w in widths],
        out_shape=[sds16 if w is None else jax.ShapeDtypeStruct((t, w), BF16) for w in widths],
        scratch_shapes=[pltpu.VMEM((tm, KV_WIDTH), F32)] * 2
        + [pltpu.VMEM((RET_WIDTH, RET_WIDTH), F32), pltpu.VMEM((8, CONV_WIDTH), F32)],
        compiler_params=pltpu.CompilerParams(dimension_semantics=("arbitrary",), vmem_limit_bytes=VMEM_LIMIT),
        name="inproj",
    )(x2d, gain, w_all, qgain, kgain, bd, cos, sin, dmask, qdec, kdec, sdec, bdm, convw)


def _cmp_kernel(xk_ref, xv_ref, pek_ref, pev_ref, w1k_ref, w1v_ref, w2k_ref, w2v_ref, kgain_ref, bd_ref,
                ok_ref, ov_ref):
    nrow = xk_ref.shape[1]
    row = lax.broadcasted_iota(jnp.int32, (nrow, KV_WIDTH), 0)

    def compress(x_ref, pe_ref, w1_ref, w2_ref):
        x = x_ref[0].astype(F32)
        first = _dot((x + pe_ref[0:1, :]).astype(BF16), w1_ref[0])
        second = _dot((x + pe_ref[1:2, :]).astype(BF16), w1_ref[1])
        pre = first + pltpu.roll(second, nrow - 1, 0)
        return _dot(jax.nn.silu(pre).astype(BF16), w2_ref[...])

    valid = row < nrow - 1
    kc = _group_rms(compress(xk_ref, pek_ref, w1k_ref, w2k_ref), bd_ref[...]) * kgain_ref[...]
    ok_ref[0] = jnp.where(valid, kc, 0.0).astype(BF16)
    vc = compress(xv_ref, pev_ref, w1v_ref, w2v_ref)
    ov_ref[0] = jnp.where(valid, vc, 0.0).astype(BF16)


def _cmp_prep(xk16, xv16, pek, pev, w1k, w1v, w2k, w2v, kgain, bd):
    b, nrow, wide = xk16.shape
    full = lambda a: pl.BlockSpec(a.shape, lambda i: (0,) * a.ndim)
    blk = pl.BlockSpec((1, nrow, wide), lambda i: (i, 0, 0))
    oblk = pl.BlockSpec((1, nrow, KV_WIDTH), lambda i: (i, 0, 0))
    return pl.pallas_call(
        _cmp_kernel,
        grid=(b,),
        in_specs=[blk, blk] + [full(a) for a in (pek, pev, w1k, w1v, w2k, w2v, kgain, bd)],
        out_specs=[oblk, oblk],
        out_shape=[jax.ShapeDtypeStruct((b, nrow, KV_WIDTH), BF16)] * 2,
        compiler_params=pltpu.CompilerParams(dimension_semantics=("arbitrary",), vmem_limit_bytes=VMEM_LIMIT),
        name="cmp_prep",
    )(xk16, xv16, pek, pev, w1k, w1v, w2k, w2v, kgain, bd)


def _nsa_kernel(q_ref, kc_ref, vc_ref, kv_ref, gl_ref, ovt_ref,
                out_ref, vtc_ref, vts_ref, vtw_ref, sel_ref, sc_ref, m_ref, acc_ref):
    i = pl.program_id(1)
    n_slc = ovt_ref.shape[0]
    tq = Q_TILE
    kc = SLC_CHUNK
    nb = kc // SLC_LEN
    s_len = kv_ref.shape[1]
    cols = NSA_HPG * tq
    groups = range(NSA_GROUPS)

    def transposed_v(v, g):
        vt = v.astype(F32).T
        other = lax.broadcasted_iota(jnp.int32, vt.shape, 0) // HEAD_DIM != g
        return jnp.where(other, 1.0, vt).astype(BF16)

    @pl.when(i == 0)
    def _():
        for g in groups:
            vtc_ref[g] = transposed_v(vc_ref[0], g)
            for c in range(s_len // kc):
                vts_ref[g, c] = transposed_v(kv_ref[0, c * kc:(c + 1) * kc, KV_WIDTH:2 * KV_WIDTH], g)
            for c in range(s_len // tq):
                vtw_ref[g, c] = transposed_v(kv_ref[0, c * tq:(c + 1) * tq, 3 * KV_WIDTH:4 * KV_WIDTH], g)

    def tile4(a):
        return jnp.concatenate([a] * NSA_HPG, axis=1)

    def colmax(a):
        return jnp.max(a, axis=0, keepdims=True)

    q = q_ref[0]
    t_q = i * tq + lax.broadcasted_iota(jnp.int32, (1, tq), 1)
    gates_t = jax.nn.sigmoid(gl_ref[0].astype(F32)).T
    q4s = [jnp.concatenate([q[:, (NSA_HPG * g + h) * LANES:(NSA_HPG * g + h + 1) * LANES] for h in range(NSA_HPG)],
                           axis=0) for g in groups]

    nwt = pl.cdiv(WIN_LEN, tq) + 1
    wtile = jnp.maximum(i - (nwt - 1), 0)
    kpos = wtile * tq + lax.broadcasted_iota(jnp.int32, (nwt * tq, tq), 0)
    wbias = tile4(jnp.where(kpos <= t_q, jnp.where(kpos > t_q - WIN_LEN, 0.0, NEG_INF), NEG_INF))
    wrows = pl.ds(pl.multiple_of(wtile * tq, tq), nwt * tq)
    cmp_end = lax.broadcasted_iota(jnp.int32, (kc_ref.shape[1], tq), 0) * CMP_STRIDE + (CMP_LEN - 1)
    cmp_bias = tile4(jnp.where(cmp_end <= t_q, 0.0, NEG_INF))
    cmp_any = tile4(jnp.where(t_q >= CMP_LEN - 1, 1.0, 0.0))

    o_cmps, o_wins = [], []
    for g in groups:
        rs = slice(g * HEAD_DIM, (g + 1) * HEAD_DIM)
        ls = (1 - g) * HEAD_DIM

        s_all = _dot_nt(jnp.concatenate([kv_ref[0, wrows, 2 * KV_WIDTH:3 * KV_WIDTH], kc_ref[0]], axis=0), q4s[g])
        sw = s_all[0:nwt * tq] + wbias
        vtw = jnp.concatenate([vtw_ref[g, wtile + j] for j in range(nwt)], axis=1)
        ow = _dot(vtw, jnp.exp((sw - colmax(sw)).astype(BF16)))
        o_wins.append(ow[rs] * (1.0 / ow[ls:ls + 1]))

        sc = s_all[nwt * tq:] + cmp_bias
        e = jnp.exp(sc - colmax(sc))
        scale = cmp_any * (1.0 / jnp.maximum(jnp.sum(e, axis=0, keepdims=True), 1e-30))
        p = e * scale
        o_cmps.append(_dot(vtc_ref[g], p.astype(BF16))[rs])
        psum = p[:, 0:tq]
        for h in range(1, NSA_HPG):
            psum = psum + p[:, h * tq:(h + 1) * tq]
        p_hi, p_lo = _split_bf16(psum)
        imp = _dot(ovt_ref[...], p_hi) + _dot(ovt_ref[...], p_lo)

        j_idx = lax.broadcasted_iota(jnp.int32, (n_slc, tq), 0)
        t_blk = (i * tq + lax.broadcasted_iota(jnp.int32, (n_slc, tq), 1)) // SLC_LEN
        forced = (j_idx == 0) | (j_idx == t_blk) | (j_idx == t_blk - 1)
        score = jnp.where(j_idx <= t_blk, imp + jnp.where(forced, FORCE_BONUS, 0.0), -1.0)
        j_f = j_idx.astype(F32)
        work = score
        selm = jnp.full((n_slc, tq), NEG_INF, F32)
        for _ in range(SLC_TOPK):
            mx = jnp.max(work, axis=0, keepdims=True)
            first = jnp.min(jnp.where(work == mx, j_f, float(n_slc)), axis=0, keepdims=True)
            hit = j_f == first
            selm = jnp.where(hit, 0.0, selm)
            work = jnp.where(hit, -2.0, work)
        sel_ref[g] = jnp.broadcast_to(selm[:, None, :], (n_slc, 8, tq))

    n_chunks = (i * tq + tq - 1) // kc + 1
    m_ref[...] = jnp.full(m_ref.shape, NEG_INF, F32)
    acc_ref[...] = jnp.zeros_like(acc_ref)
    krow = lax.broadcasted_iota(jnp.int32, (kc, tq), 0)

    def score_chunk(c):
        krows = kv_ref[0, pl.ds(pl.multiple_of(c * kc, kc), kc), 0:KV_WIDTH]
        causal = (c * kc + krow) <= t_q
        for g in groups:
            chosen = jnp.concatenate(
                [jnp.broadcast_to(sel_ref[g, c * nb + jb][None], (SLC_LEN // 8, 8, tq)).reshape(SLC_LEN, tq)
                 for jb in range(nb)], axis=0)
            sc = _dot_nt(krows, q4s[g]) + tile4(jnp.where(causal, chosen, NEG_INF))
            sc_ref[g, c] = sc
            m_ref[g] = jnp.maximum(m_ref[g], jnp.max(sc.reshape(kc // 8, 8, cols), axis=0))

    def accumulate_chunk(c, m_prev):
        m_out = []
        for g in groups:
            m_cur = colmax(m_ref[g])
            e = jnp.exp((sc_ref[g, c] - m_cur).astype(BF16))
            acc_ref[g] = acc_ref[g] * jnp.exp(m_prev[g] - m_cur) + _dot(vts_ref[g, c], e)
            m_out.append(m_cur)
        return tuple(m_out)

    def slc_trip(c, m_prev):
        m_cur = accumulate_chunk(c, m_prev)
        score_chunk(c + 1)
        return m_cur

    score_chunk(0)
    m_last = lax.fori_loop(0, n_chunks - 1, slc_trip, tuple(jnp.full((1, cols), NEG_INF, F32) for _ in groups))
    accumulate_chunk(n_chunks - 1, m_last)

    ys = []
    for g in groups:
        rs = slice(g * HEAD_DIM, (g + 1) * HEAD_DIM)
        ls = (1 - g) * HEAD_DIM
        acc = acc_ref[g]
        o_slc = acc[rs] * (1.0 / acc[ls:ls + 1])
        for h in range(NSA_HPG):
            r = (NSA_HPG * g + h) * 3
            blk = slice(h * tq, (h + 1) * tq)
            ys.append(gates_t[r:r + 1] * o_cmps[g][:, blk] + gates_t[r + 1:r + 2] * o_slc[:, blk]
                      + gates_t[r + 2:r + 3] * o_wins[g][:, blk])
    out_ref[0] = jnp.concatenate(ys, axis=0).T.astype(BF16)


def _nsa(qn, kcmp, vcmp, kv4, glog, ovt):
    b, s, _ = qn.shape
    nq = s // Q_TILE
    ncmp = kcmp.shape[1]
    n_slc = ovt.shape[0]
    cols = NSA_HPG * Q_TILE
    full = lambda a: pl.BlockSpec(a.shape, lambda bi, i: (0,) * a.ndim)
    return pl.pallas_call(
        _nsa_kernel,
        grid=(b, nq),
        in_specs=[pl.BlockSpec((1, Q_TILE, NSA_HEADS * LANES), lambda bi, i: (bi, i, 0)),
                  pl.BlockSpec((1, ncmp, KV_WIDTH), lambda bi, i: (bi, 0, 0)),
                  pl.BlockSpec((1, ncmp, KV_WIDTH), lambda bi, i: (bi, 0, 0)),
                  pl.BlockSpec((1, s, 4 * KV_WIDTH), lambda bi, i: (bi, 0, 0)),
                  pl.BlockSpec((1, Q_TILE, LANES), lambda bi, i: (bi, i, 0)),
                  full(ovt)],
        out_specs=pl.BlockSpec((1, Q_TILE, NSA_WIDTH), lambda bi, i: (bi, i, 0)),
        out_shape=jax.ShapeDtypeStruct((b, s, NSA_WIDTH), BF16),
        scratch_shapes=[pltpu.VMEM((NSA_GROUPS, KV_WIDTH, ncmp), BF16),
                        pltpu.VMEM((NSA_GROUPS, s // SLC_CHUNK, KV_WIDTH, SLC_CHUNK), BF16),
                        pltpu.VMEM((NSA_GROUPS, nq, KV_WIDTH, Q_TILE), BF16),
                        pltpu.VMEM((NSA_GROUPS, n_slc, 8, Q_TILE), F32),
                        pltpu.VMEM((NSA_GROUPS, s // SLC_CHUNK, SLC_CHUNK, cols), F32),
                        pltpu.VMEM((NSA_GROUPS, 8, cols), F32),
                        pltpu.VMEM((NSA_GROUPS, KV_WIDTH, cols), F32)],
        compiler_params=pltpu.CompilerParams(dimension_semantics=("arbitrary", "arbitrary"),
                                             vmem_limit_bytes=VMEM_LIMIT),
        name="nsa",
    )(qn, kcmp, vcmp, kv4, glog, ovt)


def _outffn_kernel(x_ref, yrc_ref, yn_ref, woa_ref, wob_ref, gain_ref, wup_ref, wdn_ref, out_ref):
    x1 = x_ref[...] + _dot(yrc_ref[...], woa_ref[...]) + _dot(yn_ref[...], wob_ref[...])
    h = x1 * lax.rsqrt(jnp.mean(x1 * x1, axis=-1, keepdims=True) + EPS) * gain_ref[...]
    h = h.astype(BF16)
    out_ref[...] = x1
    for c in range(wup_ref.shape[1] // FF_CHUNK):
        sl = slice(c * FF_CHUNK, (c + 1) * FF_CHUNK)
        u = jnp.maximum(_dot(h, wup_ref[:, sl]), 0.0)
        out_ref[...] += _dot((u * u).astype(BF16), wdn_ref[sl, :])


def _outffn(x2d, yrc, yn, woa, wob, gain, wup, wdn):
    t, d = x2d.shape
    tm = ROW_TILE
    row = lambda w: pl.BlockSpec((tm, w), lambda i: (i, 0))
    full = lambda a: pl.BlockSpec(a.shape, lambda i: (0,) * a.ndim)
    return pl.pallas_call(
        _outffn_kernel,
        grid=(t // tm,),
        in_specs=[row(d), row(yrc.shape[1]), row(yn.shape[1])] + [full(a) for a in (woa, wob, gain, wup, wdn)],
        out_specs=row(d),
        out_shape=jax.ShapeDtypeStruct((t, d), F32),
        compiler_params=pltpu.CompilerParams(dimension_semantics=("arbitrary",), vmem_limit_bytes=VMEM_LIMIT),
        name="outffn",
    )(x2d, yrc, yn, woa, wob, gain, wup, wdn)


def _block_diag_ones(n, block, value):
    idx = np.arange(n) // block
    return jnp.asarray((idx[:, None] == idx[None, :]).astype(np.float32) * value)


def _retention_tables(s):
    h, d, c = RET_HEADS, HEAD_DIM, RET_CHUNK
    pos = jnp.arange(s, dtype=F32)
    inv_freq = 1.0 / (XPOS_BASE ** jnp.linspace(0.0, 1.0, d // 2, dtype=F32))
    ang = jnp.repeat(pos[:, None] * inv_freq[None, :], 2, axis=-1)
    sign = jnp.tile(jnp.asarray([-1.0, 1.0], F32), d // 2)
    cos = jnp.tile(jnp.cos(ang), (1, h))
    sin = jnp.tile(jnp.sin(ang) * sign[None, :], (1, h))
    log_gamma = jnp.log(1.0 - 2.0 ** (-5.0 - jnp.arange(h, dtype=F32)))
    idx = jnp.arange(c, dtype=F32)
    rel = idx[:, None] - idx[None, :]
    dmask = jnp.where(rel >= 0, jnp.exp(log_gamma[:, None, None] * jnp.maximum(rel, 0.0)), 0.0)
    kdec = jnp.repeat(jnp.exp(log_gamma[:, None] * (c - 1.0 - idx)).T, d, axis=1)
    qdec = jnp.repeat(jnp.exp(log_gamma[:, None] * (idx + 1.0)).T, d, axis=1)
    bdm = _block_diag_ones(h * d, d, 1.0)
    sdec = bdm * jnp.repeat(jnp.exp(log_gamma * c), d)[:, None]
    return cos, sin, dmask, qdec, kdec, sdec, bdm


def _nsa_tables(s):
    n_rows = s // CMP_STRIDE
    n_cmp = (s - CMP_LEN) // CMP_STRIDE + 1
    n_slc = s // SLC_LEN
    cs = np.arange(n_cmp)[:, None] * CMP_STRIDE
    ss = np.arange(n_slc)[None, :] * SLC_LEN
    overlap = np.clip(np.minimum(cs + CMP_LEN, ss + SLC_LEN) - np.maximum(cs, ss), 0, None).astype(np.float32) / CMP_LEN
    ovt = np.zeros((n_slc, n_rows), np.float32)
    ovt[:, :n_cmp] = overlap.T
    return jnp.asarray(ovt, BF16)


def _cmp_weights(pe, w1, w2):
    half = CMP_LEN // 2
    eye = jnp.eye(NSA_GROUPS, dtype=F32)
    w1r = w1.reshape(2, half, HEAD_DIM, HEAD_DIM)
    w1big = jnp.einsum('arce,gh->argche', w1r, eye).reshape(2, half * KV_WIDTH, KV_WIDTH)
    pe2 = jnp.broadcast_to(pe.reshape(2, half, 1, HEAD_DIM), (2, half, NSA_GROUPS, HEAD_DIM)).reshape(2, half * KV_WIDTH)
    w2big = jnp.kron(eye, w2)
    return pe2, w1big.astype(BF16), w2big.astype(BF16)


def kernel(x, norm_mix, w_in, conv_w, nsa_q_norm, nsa_k_norm, cmp_pe_k, cmp_w1_k, cmp_w2_k,
           cmp_pe_v, cmp_w1_v, cmp_w2_v, w_out, norm_mlp, w_up, w_down):
    b, s, d = x.shape
    t = b * s
    depth = w_in.shape[0]
    ret_tabs = _retention_tables(s)
    ovt = _nsa_tables(s)
    bd128 = _block_diag_ones(KV_WIDTH, HEAD_DIM, 1.0 / HEAD_DIM).astype(BF16)
    bd256 = _block_diag_ones(RET_WIDTH, HEAD_DIM, 1.0 / HEAD_DIM).astype(BF16)
    c_rc = 4 * RET_WIDTH + 3 * CONV_WIDTH
    c_q = c_rc + NSA_WIDTH
    c_kc = c_q + KV_WIDTH
    c_vc = c_kc + KV_WIDTH
    c_kv = c_vc + 4 * KV_WIDTH
    n_gate = 3 * NSA_HEADS
    rows16 = s // CMP_STRIDE
    n_rc = RET_WIDTH + CONV_WIDTH

    xf = x.reshape(t, d)
    for l in range(depth):
        w = w_in[l].astype(BF16)
        w_all = jnp.concatenate([w[:, c_rc:c_kv], jnp.pad(w[:, c_kv:c_kv + n_gate], ((0, 0), (0, LANES - n_gate))),
                                 w[:, :c_rc]], axis=1)
        qgain = jnp.tile(nsa_q_norm[l], RET_WIDTH // HEAD_DIM)[None, :]
        kgain = jnp.tile(nsa_k_norm[l], NSA_GROUPS)[None, :]
        y_rc, qn, zkc, zvc, kv4, glog = _inproj(
            xf, norm_mix[l][None, :], w_all, qgain, jnp.tile(kgain, (1, 2)), bd256, ret_tabs, conv_w[l], s,
            4 * KV_WIDTH, LANES)
        pek, w1k, w2k = _cmp_weights(cmp_pe_k[l], cmp_w1_k[l], cmp_w2_k[l])
        pev, w1v, w2v = _cmp_weights(cmp_pe_v[l], cmp_w1_v[l], cmp_w2_v[l])
        kcmp, vcmp = _cmp_prep(zkc.reshape(b, rows16, -1), zvc.reshape(b, rows16, -1),
                               pek, pev, w1k, w1v, w2k, w2v, kgain, bd128)
        y_nsa = _nsa(qn.reshape(b, s, NSA_HEADS * LANES), kcmp, vcmp, kv4.reshape(b, s, 4 * KV_WIDTH),
                     glog.reshape(b, s, LANES), ovt)
        wo = w_out[l].astype(BF16)
        xf = _outffn(xf, y_rc.reshape(t, n_rc), y_nsa.reshape(t, NSA_WIDTH), wo[:n_rc], wo[n_rc:],
                     norm_mlp[l][None, :], w_up[l].astype(BF16), w_down[l].astype(BF16))
    return xf.reshape(b, s, d)
```
